```python
import jax, jax.numpy as jnp
from jax import lax
import numpy as np

D_MODEL = 2048
BATCH = 4
SEQ = 8192
DEPTH = 2

GRID_W = 64
CTX_LEN = 256
N_MIXERS = 2
N_A = (DEPTH + 1) // 2
N_B = DEPTH // 2
D_FF = 4 * D_MODEL
ALPHA = (2 * DEPTH) ** 0.25
BETA = (8 * DEPTH) ** -0.25
LN_EPS = 1e-5

M_HEADS = 8
M_DV = D_MODEL // M_HEADS
M_DQK = M_DV // 2
M_QK = M_HEADS * M_DQK
M_V = M_HEADS * M_DV
M_IN_COLS = 2 * M_QK + 2 * M_V + 4 * M_HEADS
M_CHUNK = 64
M_CONV = 3
M_F_BIAS_LO = 3.0
M_F_BIAS_HI = 6.0
M_NORM_EPS = 1e-6

R_HEAD = 64
R_HEADS = D_MODEL // R_HEAD
R_DECAY_LORA = 96
R_ICLR_LORA = 96
R_GATE_LORA = 256
R_GROUPS = (D_MODEL, D_MODEL, D_MODEL, R_GATE_LORA, R_DECAY_LORA, R_DECAY_LORA, R_ICLR_LORA, R_ICLR_LORA)
R_IN_COLS = 3 * D_MODEL + R_GATE_LORA + 2 * R_DECAY_LORA + 2 * R_ICLR_LORA
R_GN_EPS = 64e-5

kernel_name = 'hybrid_mlstm_rwkv7_flow_backbone'


def _offsets(sizes):
    return [int(s) for s in np.cumsum(sizes)[:-1]]


def layer_norm(x, g, b, eps=LN_EPS):
    xf = x.astype(jnp.float32)
    mu = jnp.mean(xf, -1, keepdims=True)
    var = jnp.mean(jnp.square(xf - mu), -1, keepdims=True)
    return ((xf - mu) * lax.rsqrt(var + eps) * g + b).astype(x.dtype)


def head_norm(h, eps):
    mu = jnp.mean(h, -1, keepdims=True)
    var = jnp.mean(jnp.square(h - mu), -1, keepdims=True)
    return (h - mu) * lax.rsqrt(var + eps)


def adaln(cvec, w, b):
    return jnp.split(jax.nn.silu(cvec) @ w + b, 6, axis=-1)


def centred_conv(u, w):
    pad = w.shape[0] // 2
    t = u.shape[1]
    up = jnp.pad(u, ((0, 0), (pad, pad), (0, 0)))
    return sum(w[j] * up[:, j:j + t] for j in range(w.shape[0]))


def grid_shift(u):
    bsz, t, d = u.shape
    rows = t // GRID_W
    g = u.reshape(bsz, rows, GRID_W, d)
    q = d // 4
    left = jnp.pad(g[:, :, :-1, :q], ((0, 0), (0, 0), (1, 0), (0, 0)))
    right = jnp.pad(g[:, :, 1:, q:2 * q], ((0, 0), (0, 0), (0, 1), (0, 0)))
    up = jnp.pad(g[:, :-1, :, 2 * q:3 * q], ((0, 0), (1, 0), (0, 0), (0, 0)))
    down = jnp.pad(g[:, 1:, :, 3 * q:], ((0, 0), (0, 1), (0, 0), (0, 0)))
    return jnp.concatenate([left, right, up, down], axis=-1).reshape(bsz, t, d)


def seq_shift(u):
    h = u.shape[-1] // 2
    prev = jnp.pad(u[:, :-1, :h], ((0, 0), (1, 0), (0, 0)))
    nxt = jnp.pad(u[:, 1:, h:], ((0, 0), (0, 1), (0, 0)))
    return jnp.concatenate([prev, nxt], axis=-1)


def sq_relu_mlp(u, w1, w2):
    return jnp.square(jax.nn.relu(u @ w1)) @ w2


def mlstm_chunkwise(q, k, v, ig, lf, state):
    bsz, nh, t, _ = q.shape
    nc = t // M_CHUNK

    def to_chunks(a):
        a = a.reshape(a.shape[:2] + (nc, M_CHUNK) + a.shape[3:])
        return jnp.moveaxis(a, 2, 0)

    lower = jnp.tril(jnp.ones((M_CHUNK, M_CHUNK), dtype=bool))

    def step(carry, xs):
        c_st, n_st, m_st = carry
        qc, kc, vc, ic, fc = xs
        b = jnp.cumsum(fc, axis=-1)
        dmat = jnp.where(lower, b[..., :, None] - b[..., None, :] + ic[..., None, :], -jnp.inf)
        inter = b + m_st[..., None]
        m_t = jnp.maximum(inter, jnp.max(dmat, axis=-1))
        s = jnp.einsum('bhld,bhsd->bhls', qc, kc) * jnp.exp(dmat - m_t[..., None])
        w_inter = jnp.exp(inter - m_t)
        num = jnp.einsum('bhls,bhsv->bhlv', s, vc) + w_inter[..., None] * jnp.einsum('bhvd,bhld->bhlv', c_st, qc)
        den = jnp.sum(s, axis=-1) + w_inter * jnp.einsum('bhd,bhld->bhl', n_st, qc)
        h = num / jnp.maximum(jnp.abs(den), jnp.exp(-m_t))[..., None]
        b_last = b[..., -1]
        g = b_last[..., None] - b + ic
        m_new = jnp.maximum(b_last + m_st, jnp.max(g, axis=-1))
        carry_decay = jnp.exp(b_last + m_st - m_new)
        wg = jnp.exp(g - m_new[..., None])
        c_new = carry_decay[..., None, None] * c_st + jnp.einsum('bhl,bhlv,bhld->bhvd', wg, vc, kc)
        n_new = carry_decay[..., None] * n_st + jnp.einsum('bhl,bhld->bhd', wg, kc)
        return (c_new, n_new, m_new), h

    state, hs = lax.scan(step, state, tuple(to_chunks(a) for a in (q, k, v, ig, lf)))
    hs = jnp.moveaxis(hs, 0, 2).reshape(bsz, nh, t, -1)
    return hs, state


def mlstm_mixer(u_lat, u_ctx, w_in, conv_w, gate_b, norm_w, w_out, need_ctx):
    f32 = jnp.float32
    bsz = u_lat.shape[0]

    def features(u):
        t = u.shape[1]
        p = (u @ w_in).astype(f32)
        qk, v, o, gates = jnp.split(p, [2 * M_QK, 2 * M_QK + M_V, 2 * M_QK + 2 * M_V], axis=-1)
        q, k = jnp.split(jax.nn.silu(centred_conv(qk, conv_w.astype(f32))), 2, axis=-1)
        to_heads = lambda z, d: z.reshape(bsz, t, M_HEADS, d).transpose(0, 2, 1, 3)
        g = (gates.reshape(bsz, t, 4, M_HEADS) + gate_b.astype(f32)).transpose(2, 0, 3, 1)
        fw = (g[0], jax.nn.log_sigmoid(g[1]))
        bw = (g[2], jax.nn.log_sigmoid(g[3]))
        return to_heads(q, M_DQK), to_heads(k, M_DQK) * M_DQK ** -0.5, to_heads(v, M_DV), o, fw, bw

    def flip(z):
        return jnp.flip(z, axis=2)

    def bidir(q, k, v, fw, bw, st_f, st_b):
        h_f, st_f = mlstm_chunkwise(q, k, v, fw[0], fw[1], st_f)
        h_b, st_b = mlstm_chunkwise(flip(q), flip(k), flip(v), flip(bw[0]), flip(bw[1]), st_b)
        return h_f + flip(h_b), st_f, st_b

    def readout(h, o, dtype):
        t = h.shape[2]
        hn = head_norm(h, M_NORM_EPS).transpose(0, 2, 1, 3).reshape(bsz, t, M_V) * norm_w
        return (jax.nn.sigmoid(o) * hn).astype(dtype) @ w_out

    zero = (jnp.zeros((bsz, M_HEADS, M_DV, M_DQK), f32),
            jnp.zeros((bsz, M_HEADS, M_DQK), f32),
            jnp.zeros((bsz, M_HEADS), f32))
    qc, kc, vc, oc, fwc, bwc = features(u_ctx)
    h_ctx, st_f, st_b = bidir(qc, kc, vc, fwc, bwc, zero, zero)
    ql, kl, vl, ol, fwl, bwl = features(u_lat)
    h_lat, _, _ = bidir(ql, kl, vl, fwl, bwl, st_f, st_b)
    y_lat = readout(h_lat, ol, u_lat.dtype)
    y_ctx = readout(h_ctx, oc, u_ctx.dtype) if need_ctx else None
    return y_lat, y_ctx


def rwkv7_scan(state, r, decay, k, v, a, b, reverse):
    def step(s, xs):
        r_t, w_t, k_t, v_t, a_t, b_t = xs
        sa = jnp.einsum('bhvk,bhk->bhv', s, a_t)
        s = s * w_t[:, :, None, :] + sa[..., :, None] * b_t[..., None, :] + v_t[..., :, None] * k_t[..., None, :]
        return s, jnp.einsum('bhvk,bhk->bhv', s, r_t)

    xs = tuple(jnp.moveaxis(z, 1, 0) for z in (r, decay, k, v, a, b))
    state, ys = lax.scan(step, state, xs, reverse=reverse)
    return jnp.moveaxis(ys, 0, 1), state


def rwkv7_mixer(u_lat, u_ctx, w_in, mu, g2, w0, w2, a0, a2, k_k, k_a, r_k, lnx_w, lnx_b, w_out, need_ctx):
    f32 = jnp.float32
    bsz = u_lat.shape[0]
    mu_cols = jnp.repeat(mu.T, np.array(R_GROUPS), axis=1, total_repeat_length=R_IN_COLS)
    w_cat = jnp.concatenate([w_in, w_in * mu_cols], axis=0)
    splits = _offsets(R_GROUPS)

    def features(u, shift):
        t = u.shape[1]
        heads = lambda z: z.reshape(bsz, t, R_HEADS, R_HEAD)
        p = (jnp.concatenate([u, shift(u) - u], axis=-1) @ w_cat).astype(f32)
        r, k, v, gl, wl_f, wl_b, al_f, al_b = jnp.split(p, splits, axis=-1)
        gate = jax.nn.sigmoid(gl) @ g2.astype(f32)
        kk = heads(k * k_k)
        kk = kk / jnp.maximum(jnp.linalg.norm(kk, axis=-1, keepdims=True), 1e-12)
        dirs = []
        for d, (wl, al) in enumerate(((wl_f, al_f), (wl_b, al_b))):
            log_w = -jax.nn.softplus(-(w0[d] + jnp.tanh(wl) @ w2[d])) - 0.5
            iclr = jax.nn.sigmoid(a0[d] + al @ a2[d])
            k_d = k * (1.0 + (iclr - 1.0) * k_a)
            dirs.append((heads(jnp.exp(-jnp.exp(log_w))), heads(k_d), -kk, kk * heads(iclr)))
        return heads(r), heads(v), gate, dirs

    def bidir(r, v, dirs, s_f, s_b):
        (dec_f, k_f, a_f, b_f), (dec_b, k_b, a_b, b_b) = dirs
        y_f, s_f = rwkv7_scan(s_f, r, dec_f, k_f, v, a_f, b_f, reverse=False)
        y_b, s_b = rwkv7_scan(s_b, r, dec_b, k_b, v, a_b, b_b, reverse=True)
        return y_f + y_b, s_f, s_b

    def readout(y, r, v, gate, dirs, dtype):
        t = y.shape[1]
        yn = head_norm(y, R_GN_EPS).reshape(bsz, t, D_MODEL) * lnx_w + lnx_b
        bonus = sum(jnp.sum(r * dd[1] * r_k, axis=-1, keepdims=True) * v for dd in dirs)
        return ((yn + bonus.reshape(bsz, t, D_MODEL)) * gate).astype(dtype) @ w_out

    zero = jnp.zeros((bsz, R_HEADS, R_HEAD, R_HEAD), f32)
    rc, vc, gc, dc = features(u_ctx, seq_shift)
    y_ctx, s_f, s_b = bidir(rc, vc, dc, zero, zero)
    rl, vl, gtl, dl = features(u_lat, grid_shift)
    y_lat, _, _ = bidir(rl, vl, dl, s_f, s_b)
    out_lat = readout(y_lat, rl, vl, gtl, dl, u_lat.dtype)
    out_ctx = readout(y_ctx, rc, vc, gc, dc, u_ctx.dtype) if need_ctx else None
    return out_lat, out_ctx


def setup_inputs(seed: int = 0) -> dict:
    key = jax.random.key(seed)
    ks = iter(jax.random.split(key, 40))
    f32 = jnp.float32
    D = D_MODEL

    def nrm(shape, std):
        return std * jax.random.normal(next(ks), shape, f32)

    x = nrm((BATCH, SEQ, D), 1.0)
    c = nrm((BATCH, D), 1.0)
    ctx = nrm((BATCH, CTX_LEN, D), 1.0)
    c_ctx = nrm((D,), 1.0)
    ada_w = nrm((DEPTH, D, 6 * D), D ** -0.5)
    ada_b = nrm((DEPTH, 6 * D), 0.02)
    ln_g = 1.0 + nrm((DEPTH, 2, D), 0.02)
    ln_b = nrm((DEPTH, 2, D), 0.02)
    mlp_w1 = nrm((DEPTH, D, D_FF), D ** -0.5)
    mlp_w2 = nrm((DEPTH, D_FF, D), BETA * D_FF ** -0.5)
    m_w_in = nrm((N_A, D, M_IN_COLS), D ** -0.5)
    m_conv = nrm((N_A, M_CONV, 2 * M_QK), M_CONV ** -0.5)
    i_b = nrm((N_A, 2, M_HEADS), 0.1)
    f_b = jnp.linspace(M_F_BIAS_LO, M_F_BIAS_HI, M_HEADS, dtype=f32) + nrm((N_A, 2, M_HEADS), 0.1)
    m_gate_b = jnp.stack([i_b[:, 0], f_b[:, 0], i_b[:, 1], f_b[:, 1]], axis=1)
    m_norm_w = 1.0 + nrm((N_A, M_V), 0.02)
    m_w_out = nrm((N_A, M_V, D), BETA * M_V ** -0.5)
    r_w_in = nrm((N_B, D, R_IN_COLS), D ** -0.5)
    r_mu = jax.random.uniform(next(ks), (N_B, len(R_GROUPS), D), f32)
    r_g2 = nrm((N_B, R_GATE_LORA, D), R_GATE_LORA ** -0.5)
    r_w0 = jnp.linspace(-6.0, -1.0, D, dtype=f32) + nrm((N_B, 2, D), 0.1)
    r_w2 = nrm((N_B, 2, R_DECAY_LORA, D), 0.5 * R_DECAY_LORA ** -0.5)
    r_a0 = nrm((N_B, 2, D), 0.1)
    r_a2 = nrm((N_B, 2, R_ICLR_LORA, D), 0.5 * R_ICLR_LORA ** -0.5)
    r_k_k = 0.85 + nrm((N_B, D), 0.02)
    r_k_a = 1.0 + nrm((N_B, D), 0.02)
    r_r_k = nrm((N_B, R_HEADS, R_HEAD), 0.1)
    r_lnx_w = 1.0 + nrm((N_B, D), 0.02)
    r_lnx_b = nrm((N_B, D), 0.02)
    r_w_out = nrm((N_B, D, D), BETA * D ** -0.5)
    return {'x': x, 'c': c, 'ctx': ctx, 'c_ctx': c_ctx,
            'ada_w': ada_w, 'ada_b': ada_b, 'ln_g': ln_g, 'ln_b': ln_b,
            'mlp_w1': mlp_w1, 'mlp_w2': mlp_w2,
            'm_w_in': m_w_in, 'm_conv': m_conv, 'm_gate_b': m_gate_b, 'm_norm_w': m_norm_w, 'm_w_out': m_w_out,
            'r_w_in': r_w_in, 'r_mu': r_mu, 'r_g2': r_g2, 'r_w0': r_w0, 'r_w2': r_w2, 'r_a0': r_a0, 'r_a2': r_a2,
            'r_k_k': r_k_k, 'r_k_a': r_k_a, 'r_r_k': r_r_k, 'r_lnx_w': r_lnx_w, 'r_lnx_b': r_lnx_b,
            'r_w_out': r_w_out}


def reference(x, c, ctx, c_ctx, ada_w, ada_b, ln_g, ln_b, mlp_w1, mlp_w2,
              m_w_in, m_conv, m_gate_b, m_norm_w, m_w_out,
              r_w_in, r_mu, r_g2, r_w0, r_w2, r_a0, r_a2, r_k_k, r_k_a, r_r_k,
              r_lnx_w, r_lnx_b, r_w_out):
    h_lat, h_ctx = x, ctx
    for i in range(DEPTH):
        need_ctx = i < DEPTH - 1
        sh1, sc1, g1, sh2, sc2, g2 = [z[:, None, :] for z in adaln(c, ada_w[i], ada_b[i])]
        csh1, csc1, cg1, csh2, csc2, cg2 = adaln(c_ctx, ada_w[i], ada_b[i])
        u_lat = h_lat * (1.0 + sc1) + sh1
        u_ctx = h_ctx * (1.0 + csc1) + csh1
        j = i // N_MIXERS
        if i % N_MIXERS == 0:
            y_lat, y_ctx = mlstm_mixer(u_lat, u_ctx, m_w_in[j], m_conv[j], m_gate_b[j], m_norm_w[j], m_w_out[j], need_ctx)
        else:
            y_lat, y_ctx = rwkv7_mixer(u_lat, u_ctx, r_w_in[j], r_mu[j], r_g2[j], r_w0[j], r_w2[j], r_a0[j], r_a2[j],
                                       r_k_k[j], r_k_a[j], r_r_k[j], r_lnx_w[j], r_lnx_b[j], r_w_out[j], need_ctx)
        h_lat = layer_norm(ALPHA * h_lat + g1 * y_lat, ln_g[i, 0], ln_b[i, 0])
        f_lat = sq_relu_mlp(h_lat * (1.0 + sc2) + sh2, mlp_w1[i], mlp_w2[i])
        h_lat = layer_norm(ALPHA * h_lat + g2 * f_lat, ln_g[i, 1], ln_b[i, 1])
        if need_ctx:
            h_ctx = layer_norm(ALPHA * h_ctx + cg1 * y_ctx, ln_g[i, 0], ln_b[i, 0])
            f_ctx = sq_relu_mlp(h_ctx * (1.0 + csc2) + csh2, mlp_w1[i], mlp_w2[i])
            h_ctx = layer_norm(ALPHA * h_ctx + cg2 * f_ctx, ln_g[i, 1], ln_b[i, 1])
    return h_lat
```

```python
import functools

import jax
import jax.numpy as jnp
from jax import lax
from jax.experimental import pallas as pl
from jax.experimental.pallas import tpu as pltpu

F32 = jnp.float32
BF16 = jnp.bfloat16

DEPTH = 2
GRID_W = 64
ALPHA = (2 * DEPTH) ** 0.25
LN_EPS = 1e-5
M_HEADS = 8
M_DQK = 128
M_DV = 256
M_NORM_EPS = 1e-6
R_HEAD = 64
R_GN_EPS = 64e-5
R_LORA = 96
R_LORA_PAD = 128
R_GATE_LORA = 256

TM = 512
TMO = 256
M_CHUNK = 256
R_CHUNK = 64
R_BLOCK = 256
R_GROUP = 256
FF_TILE = 1024
MOD_ROWS = 8
NEG = -1e30
VMEM_LIMIT = 52 * 1024 * 1024


def _cp(*sem):
    return pltpu.CompilerParams(dimension_semantics=sem, vmem_limit_bytes=VMEM_LIMIT)


def _dot(a, b):
    return jnp.dot(a, b, preferred_element_type=F32)


def _dot_nt(a, b):
    return lax.dot_general(a, b, (((1,), (1,)), ((), ())), preferred_element_type=F32)


def _dot_tn(a, b):
    return lax.dot_general(a, b, (((0,), (0,)), ((), ())), preferred_element_type=F32)


def _split2(x):
    hi = x.astype(BF16)
    lo = (x - hi.astype(F32)).astype(BF16)
    return hi, lo


def _split3(x):
    hi = x.astype(BF16)
    r = x - hi.astype(F32)
    mid = r.astype(BF16)
    lo = (r - mid.astype(F32)).astype(BF16)
    return hi, mid, lo


def _sigmoid(x):
    return 1.0 / (1.0 + jnp.exp(-x))


def _log_sigmoid(x):
    return jnp.minimum(x, 0.0) - jnp.log(1.0 + jnp.exp(-jnp.abs(x)))


def _layer_norm(z, g, b):
    mu = jnp.mean(z, axis=-1, keepdims=True)
    zc = z - mu
    var = jnp.mean(zc * zc, axis=-1, keepdims=True)
    return zc * lax.rsqrt(var + LN_EPS) * g + b


def _iota(shape, dim):
    return lax.broadcasted_iota(jnp.int32, shape, dim)


def _group_ones(n, group):
    return jnp.where(_iota((n, n), 0) // group == _iota((n, n), 1) // group, 1.0, 0.0).astype(BF16)


def _ada_body(c_ref, w_ref, b_ref, o_ref):
    x = c_ref[...]
    x = x * _sigmoid(x)
    xh, xl = _split2(x)
    wh, wl = _split2(w_ref[...])
    o_ref[...] = _dot(xh, wh) + _dot(xh, wl) + _dot(xl, wh) + b_ref[...]


def _ada(cvec, ada_w, ada_b):
    depth, d, n = ada_w.shape
    tn = 1024
    return pl.pallas_call(
        _ada_body,
        grid=(depth, n // tn),
        in_specs=[pl.BlockSpec((MOD_ROWS, d), lambda l, j: (0, 0)),
                  pl.BlockSpec((None, d, tn), lambda l, j: (l, 0, j)),
                  pl.BlockSpec((None, 1, tn), lambda l, j: (l, 0, j))],
        out_specs=pl.BlockSpec((None, MOD_ROWS, tn), lambda l, j: (l, 0, j)),
        out_shape=jax.ShapeDtypeStruct((depth, MOD_ROWS, n), F32),
        compiler_params=_cp("parallel", "parallel"),
        name="ada_mod",
    )(cvec, ada_w, ada_b.reshape(depth, 1, n))


def _mod_spec(d, which, row_fn):
    return pl.BlockSpec((None, 1, d), lambda *g: (row_fn(*g), 0, which))


def _proj_body(h_ref, sc_ref, sh_ref, w_ref, o_ref, u_ref):
    @pl.when(pl.program_id(2) == 0)
    def _():
        u_ref[...] = (h_ref[...] * (1.0 + sc_ref[...]) + sh_ref[...]).astype(BF16)

    o_ref[...] = _dot(u_ref[...], w_ref[...])


def _proj(h, mod, w, tn, batch, ctx_tile, name):
    b, tt, d = h.shape
    n = w.shape[1]
    row = lambda bi, i, j: jnp.where(i == ctx_tile, batch, bi)
    return pl.pallas_call(
        _proj_body,
        grid=(b, tt // TM, n // tn),
        in_specs=[pl.BlockSpec((None, TM, d), lambda bi, i, j: (bi, i, 0)),
                  _mod_spec(d, 1, row), _mod_spec(d, 0, row),
                  pl.BlockSpec((d, tn), lambda bi, i, j: (0, j))],
        out_specs=pl.BlockSpec((None, TM, tn), lambda bi, i, j: (bi, i, j)),
        out_shape=jax.ShapeDtypeStruct((b, tt, n), F32),
        scratch_shapes=[pltpu.VMEM((TM, d), BF16)],
        compiler_params=_cp("parallel", "parallel", "arbitrary"),
        name=name,
    )(h, mod, mod, w)


def _seq_block(c, direction, n_lat, n_ctx):
    if direction == 0:
        return jnp.where(c < n_ctx, n_lat + c, c - n_ctx)
    return jnp.where(c < n_ctx, n_lat + n_ctx - 1 - c, n_lat - 1 - (c - n_ctx))


def _mlstm_body(direction, n_lat, n_ctx,
                qk_ref, qkp_ref, qkn_ref, v_ref, g_ref, gt_ref, cw_ref, gb_ref, gbt_ref,
                o_ref, ct_ref, n_ref, m_ref):
    L = M_CHUNK
    c = pl.program_id(1)
    n_live = n_lat + n_ctx

    @pl.when(c == 0)
    def _():
        ct_ref[...] = jnp.zeros_like(ct_ref)
        n_ref[...] = jnp.zeros_like(n_ref)
        m_ref[...] = jnp.zeros_like(m_ref)

    @pl.when(c >= n_live)
    def _():
        o_ref[...] = jnp.zeros_like(o_ref)

    @pl.when(c < n_live)
    def _():
        blk = _seq_block(c, direction, n_lat, n_ctx)
        first = jnp.logical_or(blk == 0, blk == n_lat)
        last = jnp.logical_or(blk == n_lat - 1, blk == n_live - 1)
        p = qk_ref[...]
        rows = _iota((L, 1), 0)
        prev_row = jnp.where(first, 0.0, qkp_ref[7:8, :])
        next_row = jnp.where(last, 0.0, qkn_ref[0:1, :])
        pm = jnp.where(rows == 0, prev_row, pltpu.roll(p, 1, 0))
        pp = jnp.where(rows == L - 1, next_row, pltpu.roll(p, L - 1, 0))
        qkc = cw_ref[0:1, :] * pm + cw_ref[1:2, :] * p + cw_ref[2:3, :] * pp
        qk = qkc * _sigmoid(qkc)
        nq = M_HEADS * M_DQK
        q_all = qk[:, :nq]
        k_all = qk[:, nq:] * (M_DQK ** -0.5)
        v_all = v_ref[...]

        g = g_ref[...] + gb_ref[...]
        gt = gt_ref[...] + gbt_ref[...]
        r_i = _iota((L, L), 0)
        c_i = _iota((L, L), 1)
        causal = (r_i >= c_i) if direction == 0 else (r_i <= c_i)
        tri = jnp.where(causal, 1.0, 0.0).astype(BF16)
        tri_t = jnp.where(causal, 0.0, 1.0).astype(BF16) + jnp.where(r_i == c_i, 1.0, 0.0).astype(BF16)
        lf = _log_sigmoid(g)
        lft = _log_sigmoid(gt)
        b_col = sum(_dot(tri, part) for part in _split3(lf))
        b_row = sum(_dot(part, tri_t) for part in _split3(lft))
        e_last = L - 1 if direction == 0 else 0

        for hd in range(M_HEADS):
            ci = 16 * direction + hd
            cf = ci + 8
            b_c = b_col[:, cf:cf + 1]
            i_c = g[:, ci:ci + 1]
            b_r = b_row[cf:cf + 1, :]
            i_r = gt[ci:ci + 1, :]
            m_st = m_ref[hd][0:1, 0:1]
            n_st = n_ref[hd][0:1, :]
            q_h = q_all[:, hd * M_DQK:(hd + 1) * M_DQK]
            k_h = k_all[:, hd * M_DQK:(hd + 1) * M_DQK]
            v_h = v_all[:, hd * M_DV:(hd + 1) * M_DV]
            q_b = q_h.astype(BF16)
            k_b = k_h.astype(BF16)

            dmat = jnp.where(causal, b_c - b_r + i_r, NEG)
            inter = b_c + m_st
            m_t = jnp.maximum(inter, jnp.max(dmat, axis=-1, keepdims=True))
            s = _dot_nt(q_b, k_b) * jnp.exp(dmat - m_t)
            w_inter = jnp.exp(inter - m_t)
            num = _dot(s.astype(BF16), v_h.astype(BF16)) + w_inter * _dot(q_b, ct_ref[hd].astype(BF16))
            den = jnp.sum(s, axis=-1, keepdims=True) + w_inter * jnp.sum(q_h * n_st, axis=-1, keepdims=True)
            o_ref[:, hd * M_DV:(hd + 1) * M_DV] = num / jnp.maximum(jnp.abs(den), jnp.exp(-m_t))

            b_last = b_c[e_last:e_last + 1, :]
            g_c = b_last - b_c + i_c
            g_r = b_last - b_r + i_r
            m_new = jnp.maximum(b_last + m_st, jnp.max(g_r, axis=-1, keepdims=True))
            decay = jnp.exp(b_last + m_st - m_new)
            wg_c = jnp.exp(g_c - m_new)
            wg_r = jnp.exp(g_r - m_new)
            ct_ref[hd] = decay * ct_ref[hd] + _dot_tn(k_b, (wg_c * v_h).astype(BF16))
            n_new = decay * n_st + _dot(jnp.broadcast_to(wg_r, (8, L)).astype(BF16), k_b)[0:1, :]
            n_ref[hd] = jnp.broadcast_to(n_new, (8, M_DQK))
            m_ref[hd] = jnp.broadcast_to(m_new, (8, 128))


def _mlstm_scan(p, gates, gates_t, conv_w, gb_row, gb_col, direction, s_len, c_len):
    b, tt, _ = p.shape
    L = M_CHUNK
    n_lat, n_ctx = s_len // L, c_len // L
    n_all = tt // L
    nqk = 2 * M_HEADS * M_DQK
    nv = M_HEADS * M_DV

    def blk(c):
        return jnp.where(c < n_lat + n_ctx, _seq_block(c, direction, n_lat, n_ctx), c)

    r8 = L // 8
    return pl.pallas_call(
        functools.partial(_mlstm_body, direction, n_lat, n_ctx),
        grid=(b, n_all),
        in_specs=[pl.BlockSpec((None, L, nqk), lambda bi, c: (bi, blk(c), 0)),
                  pl.BlockSpec((None, 8, nqk), lambda bi, c: (bi, jnp.maximum(blk(c) * r8 - 1, 0), 0)),
                  pl.BlockSpec((None, 8, nqk), lambda bi, c: (bi, jnp.minimum((blk(c) + 1) * r8, tt // 8 - 1), 0)),
                  pl.BlockSpec((None, L, nv), lambda bi, c: (bi, blk(c), 1)),
                  pl.BlockSpec((None, L, 128), lambda bi, c: (bi, blk(c), 0)),
                  pl.BlockSpec((None, 32, L), lambda bi, c: (bi, 0, blk(c))),
                  pl.BlockSpec((3, nqk), lambda bi, c: (0, 0)),
                  pl.BlockSpec((1, 128), lambda bi, c: (0, 0)),
                  pl.BlockSpec((32, 1), lambda bi, c: (0, 0))],
        out_specs=pl.BlockSpec((None, L, nv), lambda bi, c: (bi, blk(c), 0)),
        out_shape=jax.ShapeDtypeStruct((b, tt, nv), F32),
        scratch_shapes=[pltpu.VMEM((M_HEADS, M_DQK, M_DV), F32),
                        pltpu.VMEM((M_HEADS, 8, M_DQK), F32),
                        pltpu.VMEM((M_HEADS, 8, 128), F32)],
        compiler_params=_cp("parallel", "arbitrary"),
        name="mlstm_scan_%d" % direction,
    )(p, p, p, p, gates, gates_t, conv_w, gb_row, gb_col)


def _out_epilogue(z_b, w_ref, res_ref, gate_ref, lg_ref, lb_ref, o_ref):
    y = _dot(z_b, w_ref[...])
    o_ref[...] = _layer_norm(ALPHA * res_ref[...] + gate_ref[...] * y, lg_ref[...], lb_ref[...])


def _mlstm_out_body(hf_ref, hb_ref, og_ref, nw_ref, w_ref, res_ref, gate_ref, lg_ref, lb_ref, o_ref, z_ref):
    ones = jnp.full((M_DV, M_DV), 1.0, BF16)
    for hd in range(M_HEADS):
        sl = slice(hd * M_DV, (hd + 1) * M_DV)
        hh = hf_ref[:, sl] + hb_ref[:, sl]
        mu = _dot(hh.astype(BF16), ones) * (1.0 / M_DV)
        hc = hh - mu
        var = _dot((hc * hc).astype(BF16), ones) * (1.0 / M_DV)
        hn = hc * lax.rsqrt(var + M_NORM_EPS) * nw_ref[:, sl]
        z_ref[:, sl] = (_sigmoid(og_ref[:, sl]) * hn).astype(BF16)
    _out_epilogue(z_ref[...], w_ref, res_ref, gate_ref, lg_ref, lb_ref, o_ref)


def _rwkv_out_body(yf_ref, yb_ref, y0_ref, bon_ref, gt_ref, lw_ref, lbx_ref,
                   w_ref, res_ref, gate_ref, lg_ref, lb_ref, o_ref, z_ref):
    ones = _group_ones(R_GROUP, R_HEAD)
    d = o_ref.shape[-1]
    for gi in range(d // R_GROUP):
        sl = slice(gi * R_GROUP, (gi + 1) * R_GROUP)
        y = yf_ref[:, sl] + yb_ref[:, sl] + y0_ref[:, sl]
        mu = _dot(y.astype(BF16), ones) * (1.0 / R_HEAD)
        yc = y - mu
        var = _dot((yc * yc).astype(BF16), ones) * (1.0 / R_HEAD)
        yn = yc * lax.rsqrt(var + R_GN_EPS) * lw_ref[:, sl] + lbx_ref[:, sl]
        z_ref[:, sl] = ((yn + bon_ref[:, sl]) * gt_ref[:, sl]).astype(BF16)
    _out_epilogue(z_ref[...], w_ref, res_ref, gate_ref, lg_ref, lb_ref, o_ref)


def _out_call(body, streams, stream_col, vecs, w, res, mod, ln_g, ln_b, batch, s_len, n_tiles, name):
    b, _, d = res.shape
    row = lambda bi, i: jnp.where(i * TMO >= s_len, batch, bi)
    tile = lambda col: pl.BlockSpec((None, TMO, d), lambda bi, i: (bi, i, col))
    vec = pl.BlockSpec((1, d), lambda bi, i: (0, 0))
    return pl.pallas_call(
        body,
        grid=(b, n_tiles),
        in_specs=[tile(col) for col in stream_col] + [vec] * len(vecs)
                 + [pl.BlockSpec((d, d), lambda bi, i: (0, 0)), tile(0), _mod_spec(d, 2, row), vec, vec],
        out_specs=tile(0),
        out_shape=jax.ShapeDtypeStruct((b, n_tiles * TMO, d), F32),
        scratch_shapes=[pltpu.VMEM((TMO, d), BF16)],
        compiler_params=_cp("parallel", "parallel"),
        name=name,
    )(*streams, *vecs, w, res, mod, ln_g, ln_b)


def _mlp_body(h_ref, sc_ref, sh_ref, gate_ref, w1_ref, w2_ref, lg_ref, lb_ref, o_ref, u_ref, acc_ref):
    k = pl.program_id(2)

    @pl.when(k == 0)
    def _():
        u_ref[...] = (h_ref[...] * (1.0 + sc_ref[...]) + sh_ref[...]).astype(BF16)
        acc_ref[...] = jnp.zeros_like(acc_ref)

    a = jnp.maximum(_dot(u_ref[...], w1_ref[...]), 0.0)
    acc_ref[...] += _dot((a * a).astype(BF16), w2_ref[...])

    @pl.when(k == pl.num_programs(2) - 1)
    def _():
        o_ref[...] = _layer_norm(ALPHA * h_ref[...] + gate_ref[...] * acc_ref[...], lg_ref[...], lb_ref[...])


def _mlp(h, mod, w1, w2, ln_g, ln_b, batch, ctx_tile, n_tiles, name):
    b, _, d = h.shape
    dff = w1.shape[1]
    row = lambda bi, i, k: jnp.where(i == ctx_tile, batch, bi)
    tile = pl.BlockSpec((None, TM, d), lambda bi, i, k: (bi, i, 0))
    vec = pl.BlockSpec((1, d), lambda bi, i, k: (0, 0))
    return pl.pallas_call(
        _mlp_body,
        grid=(b, n_tiles, dff // FF_TILE),
        in_specs=[tile, _mod_spec(d, 4, row), _mod_spec(d, 3, row), _mod_spec(d, 5, row),
                  pl.BlockSpec((d, FF_TILE), lambda bi, i, k: (0, k)),
                  pl.BlockSpec((FF_TILE, d), lambda bi, i, k: (k, 0)), vec, vec],
        out_specs=tile,
        out_shape=jax.ShapeDtypeStruct((b, n_tiles * TM, d), F32),
        scratch_shapes=[pltpu.VMEM((TM, d), BF16), pltpu.VMEM((TM, d), F32)],
        compiler_params=_cp("parallel", "parallel", "arbitrary"),
        name=name,
    )(h, mod, mod, mod, w1, w2, ln_g, ln_b)


def _rproj_body(ctx_tile, c_len, n_lat_tiles,
                h_ref, hp_ref, hn_ref, sc_ref, sh_ref, mu_ref, w_ref, o_ref, u_ref, xx_ref):
    i = pl.program_id(1)

    @pl.when(pl.program_id(2) == 0)
    def _():
        is_ctx = i == ctx_tile
        sc = 1.0 + sc_ref[...]
        sh = sh_ref[...]
        u = h_ref[...] * sc + sh
        u_prev = hp_ref[...] * sc + sh
        u_next = hn_ref[...] * sc + sh
        u_ref[...] = u
        t = _iota((TM, 1), 0)
        tw = jnp.bitwise_and(t, GRID_W - 1)
        q = u.shape[1] // 4
        flag = lambda cond: jnp.where(cond, 1, 0)
        ctx_prev, ctx_next = flag(t != 0), flag(t != c_len - 1)
        lat_left, lat_right = flag(tw != 0), flag(tw != GRID_W - 1)
        lat_up = jnp.where(i == 0, flag(t >= GRID_W), 1)
        lat_down = jnp.where(i == n_lat_tiles - 1, flag(t < TM - GRID_W), 1)
        for qi in range(4):
            sl = slice(qi * q, (qi + 1) * q)
            uq = u[:, sl]
            before = pltpu.roll(uq, 1, 0)
            after = pltpu.roll(uq, TM - 1, 0)
            if qi == 0:
                src = before
                keep = jnp.where(is_ctx, ctx_prev, lat_left)
            elif qi == 1:
                src = jnp.where(is_ctx, before, after)
                keep = jnp.where(is_ctx, ctx_prev, lat_right)
            elif qi == 2:
                above = jnp.concatenate([u_prev[:, sl], uq[:TM - GRID_W]], axis=0)
                src = jnp.where(is_ctx, after, above)
                keep = jnp.where(is_ctx, ctx_next, lat_up)
            else:
                below = jnp.concatenate([uq[GRID_W:], u_next[:, sl]], axis=0)
                src = jnp.where(is_ctx, after, below)
                keep = jnp.where(is_ctx, ctx_next, lat_down)
            xx_ref[:, sl] = jnp.where(keep != 0, src, 0.0) - uq

    x = (u_ref[...] + xx_ref[...] * mu_ref[...]).astype(BF16)
    o_ref[...] = _dot(x, w_ref[...])


def _rproj(h, mod, mu, w, tn, mu_of_tile, batch, s_len, c_len, name):
    b, tt, d = h.shape
    n = w.shape[1]
    ctx_tile = s_len // TM
    row = lambda bi, i, j: jnp.where(i == ctx_tile, batch, bi)
    r64 = TM // GRID_W
    return pl.pallas_call(
        functools.partial(_rproj_body, ctx_tile, c_len, s_len // TM),
        grid=(b, tt // TM, n // tn),
        in_specs=[pl.BlockSpec((None, TM, d), lambda bi, i, j: (bi, i, 0)),
                  pl.BlockSpec((None, GRID_W, d), lambda bi, i, j: (bi, jnp.maximum(i * r64 - 1, 0), 0)),
                  pl.BlockSpec((None, GRID_W, d),
                               lambda bi, i, j: (bi, jnp.minimum((i + 1) * r64, tt // GRID_W - 1), 0)),
                  _mod_spec(d, 1, row), _mod_spec(d, 0, row),
                  pl.BlockSpec((None, 1, d), lambda bi, i, j: (mu_of_tile(j), 0, 0)),
                  pl.BlockSpec((d, tn), lambda bi, i, j: (0, j))],
        out_specs=pl.BlockSpec((None, TM, tn), lambda bi, i, j: (bi, i, j)),
        out_shape=jax.ShapeDtypeStruct((b, tt, n), F32),
        scratch_shapes=[pltpu.VMEM((TM, d), F32), pltpu.VMEM((TM, d), F32)],
        compiler_params=_cp("parallel", "parallel", "arbitrary"),
        name=name,
    )(h, h, h, mod, mod, mu, w)


def _block_diag(x, mask):
    return jnp.where(mask, jnp.concatenate([x] * (R_GROUP // R_CHUNK), axis=0), 0.0)


def _rloc_body(r_ref, k_ref, v_ref, lo_ref, g2_ref, w2_ref, a2_ref, vec_ref,
               rf_ref, rb_ref, y0_ref, bon_ref, gate_ref, gf_ref, gb_ref, hf_ref, hb_ref,
               rs_ref, vs_ref, kk_ref, lw_ref, kd_ref, bb_ref):
    L = R_CHUNK
    W = R_GROUP
    lo = lo_ref[...]
    r = r_ref[...]
    k = k_ref[...]
    v = v_ref[...]
    w0 = (vec_ref[0:1, :], vec_ref[1:2, :])
    a0 = (vec_ref[2:3, :], vec_ref[3:4, :])
    k_k = vec_ref[4:5, :]
    k_a = vec_ref[5:6, :]
    r_k = vec_ref[6:7, :]
    ones = _group_ones(W, R_HEAD)

    gate_ref[...] = _dot(_sigmoid(lo[:, :R_GATE_LORA]).astype(BF16), g2_ref[...])
    kraw = k * k_k
    ss = _dot((kraw * kraw).astype(BF16), ones)
    kk = kraw / jnp.maximum(jnp.sqrt(ss), 1e-12)
    rs_ref[...] = r
    vs_ref[...] = v
    kk_ref[...] = kk
    kd_sum = jnp.zeros_like(k)
    for d in range(2):
        wl = lo[:, R_GATE_LORA + d * R_LORA_PAD:R_GATE_LORA + (d + 1) * R_LORA_PAD]
        al = lo[:, R_GATE_LORA + (2 + d) * R_LORA_PAD:R_GATE_LORA + (3 + d) * R_LORA_PAD]
        lwx = w0[d] + _dot(jnp.tanh(wl).astype(BF16), w2_ref[d])
        lw_ref[d] = -_sigmoid(lwx) * (jnp.e ** -0.5)
        iclr = _sigmoid(a0[d] + _dot(al.astype(BF16), a2_ref[d]))
        kd = k * (1.0 + (iclr - 1.0) * k_a)
        kd_ref[d] = kd
        bb_ref[d] = kk * iclr
        kd_sum = kd_sum + kd
    bon_ref[...] = _dot((r * r_k * kd_sum).astype(BF16), ones) * v

    lane_head = _iota((W, W), 1) // R_HEAD
    bd_mask = _iota((W, W), 0) // L == lane_head
    t_i = _iota((L, W), 0)
    s_i = jnp.bitwise_and(_iota((L, W), 1), L - 1)
    eye = jnp.where(t_i == s_i, 1.0, 0.0)
    ts_xor = jnp.bitwise_xor(t_i, s_i)
    diag2 = jnp.right_shift(ts_xor, 1) == 0
    off_masks = [jnp.right_shift(ts_xor, lv) == 1 for lv in range(1, 6)]
    head_of_lane = _iota((L, W), 1) // R_HEAD
    r_i = _iota((L, L), 0)
    c_i = _iota((L, L), 1)
    outs = ((rf_ref, gf_ref, hf_ref), (rb_ref, gb_ref, hb_ref))

    def chunk(ci, carry):
        rows = pl.ds(pl.multiple_of(ci * L, L), L)
        rc = rs_ref[rows, :]
        vc = vs_ref[rows, :]
        kkc = kk_ref[rows, :]
        bd_v = _block_diag(vc, bd_mask).astype(BF16)
        y0 = jnp.zeros((L, W), F32)
        for d in range(2):
            rp_ref, g_ref, h_ref = outs[d]
            lwc = lw_ref[d, rows, :]
            kdc = kd_ref[d, rows, :]
            bc = bb_ref[d, rows, :]
            if d == 0:
                tri = jnp.where(r_i >= c_i, 1.0, 0.0).astype(BF16)
                strict, incl, e_last = t_i > s_i, t_i >= s_i, L - 1
            else:
                tri = jnp.where(r_i <= c_i, 1.0, 0.0).astype(BF16)
                strict, incl, e_last = t_i < s_i, t_i <= s_i, 0
            cum = sum(_dot(tri, part) for part in _split3(lwc))
            tot = cum[e_last:e_last + 1, :]
            p_inv = jnp.exp(-cum)
            p_end = jnp.exp(tot - cum)
            a_t = -kkc * jnp.exp(cum - lwc)
            r_t = rc * jnp.exp(cum)
            z = _dot_nt(jnp.concatenate([a_t, r_t], axis=0).astype(BF16),
                        jnp.concatenate([_block_diag(bc * p_inv, bd_mask),
                                         _block_diag(kdc * p_inv, bd_mask)], axis=0).astype(BF16))
            a_ab = jnp.where(strict, z[:L, :W], 0.0)
            a_ak = jnp.where(strict, z[:L, W:], 0.0)
            m_rb = jnp.where(incl, z[L:, :W], 0.0)
            m_rk = jnp.where(incl, z[L:, W:], 0.0)
            tinv = eye + jnp.where(diag2, a_ab, 0.0)
            for off in off_masks:
                x1 = _dot(tinv.astype(BF16), _block_diag(jnp.where(off, a_ab, 0.0), bd_mask).astype(BF16))
                tinv = tinv + _dot(x1.astype(BF16), _block_diag(tinv, bd_mask).astype(BF16))
            akv = _dot(a_ak.astype(BF16), bd_v)
            x = _dot(tinv.astype(BF16),
                     jnp.concatenate([_block_diag(a_t, bd_mask), _block_diag(akv, bd_mask)], axis=1).astype(BF16))
            a_p = x[:, :W]
            u0 = x[:, W:]
            y2 = _dot(m_rb.astype(BF16),
                      jnp.concatenate([_block_diag(a_p, bd_mask), _block_diag(u0, bd_mask)], axis=1).astype(BF16))
            rp_ref[rows, :] = (r_t + y2[:, :W]).astype(BF16)
            y0 = y0 + y2[:, W:] + _dot(m_rk.astype(BF16), bd_v)
            zs = _dot_tn((bc * p_end).astype(BF16), jnp.concatenate([a_p, u0], axis=1).astype(BF16))
            zk = _dot_tn((kdc * p_end).astype(BF16), vc.astype(BF16))
            g_full = zs[:, :W]
            h_full = zs[:, W:] + zk
            g_sbs = eye * jnp.exp(tot)
            h_sbs = jnp.zeros((L, W), F32)
            for hd in range(W // R_HEAD):
                sel = head_of_lane == hd
                g_sbs = g_sbs + jnp.where(sel, g_full[hd * R_HEAD:(hd + 1) * R_HEAD, :], 0.0)
                h_sbs = h_sbs + jnp.where(sel, h_full[hd * R_HEAD:(hd + 1) * R_HEAD, :], 0.0)
            g_ref[rows, :] = g_sbs
            h_ref[rows, :] = h_sbs
        y0_ref[rows, :] = y0
        return carry

    lax.fori_loop(0, R_BLOCK // L, chunk, 0)


def _rloc(rkv, lora, g2, w2p, a2p, vecs, s_len, c_len):
    b, tt, n3 = rkv.shape
    d = n3 // 3
    nb = (s_len + c_len) // R_BLOCK
    ng = d // R_GROUP
    tok = lambda col: pl.BlockSpec((None, R_BLOCK, R_GROUP), lambda bi, i, j: (bi, i, col * ng + j))
    out_tok = pl.BlockSpec((None, R_BLOCK, R_GROUP), lambda bi, i, j: (bi, i, j))
    f32_out = jax.ShapeDtypeStruct((b, tt, d), F32)
    bf_out = jax.ShapeDtypeStruct((b, tt, d), BF16)
    sc = lambda *shape: pltpu.VMEM(shape, F32)
    return pl.pallas_call(
        _rloc_body,
        grid=(b, nb, ng),
        in_specs=[tok(0), tok(1), tok(2),
                  pl.BlockSpec((None, R_BLOCK, lora.shape[2]), lambda bi, i, j: (bi, i, 0)),
                  pl.BlockSpec((R_GATE_LORA, R_GROUP), lambda bi, i, j: (0, j)),
                  pl.BlockSpec((2, R_LORA_PAD, R_GROUP), lambda bi, i, j: (0, 0, j)),
                  pl.BlockSpec((2, R_LORA_PAD, R_GROUP), lambda bi, i, j: (0, 0, j)),
                  pl.BlockSpec((8, R_GROUP), lambda bi, i, j: (0, j))],
        out_specs=[out_tok] * 9,
        out_shape=[bf_out, bf_out, f32_out, f32_out, f32_out, f32_out, f32_out, f32_out, f32_out],
        scratch_shapes=[sc(R_BLOCK, R_GROUP), sc(R_BLOCK, R_GROUP), sc(R_BLOCK, R_GROUP),
                        sc(2, R_BLOCK, R_GROUP), sc(2, R_BLOCK, R_GROUP), sc(2, R_BLOCK, R_GROUP)],
        compiler_params=_cp("parallel", "parallel", "arbitrary"),
        name="rwkv_local",
    )(rkv, rkv, rkv, lora, g2, w2p, a2p, vecs)


def _rseq_body(gf_ref, gb_ref, hf_ref, hb_ref, rf_ref, rb_ref, yf_ref, yb_ref, st_ref):
    L = R_CHUNK
    W = R_GROUP

    @pl.when(pl.program_id(1) == 0)
    def _():
        st_ref[...] = jnp.zeros_like(st_ref)

    bd_mask = _iota((W, W), 0) // L == _iota((W, W), 1) // R_HEAD
    ins = ((gf_ref, hf_ref, rf_ref, yf_ref), (gb_ref, hb_ref, rb_ref, yb_ref))
    for d in range(2):
        g_ref, ha_ref, rp_ref, y_ref = ins[d]
        for gi in range(st_ref.shape[-1] // W):
            sl = slice(gi * W, (gi + 1) * W)
            st = st_ref[d, :, sl]
            st_hi = st.astype(BF16)
            st_lo = (st - st_hi.astype(F32)).astype(BF16)
            g_hi, g_lo = _split2(g_ref[:, sl])
            rp = rp_ref[:, sl]
            bd_hi = _block_diag(st_hi.astype(F32), bd_mask).astype(BF16)
            bd_lo = _block_diag(st_lo.astype(F32), bd_mask).astype(BF16)
            o1 = _dot(jnp.concatenate([g_hi, g_lo, rp], axis=0), bd_hi)
            o2 = _dot(jnp.concatenate([g_hi, rp], axis=0), bd_lo)
            st_ref[d, :, sl] = o1[:L] + o1[L:2 * L] + o2[:L] + ha_ref[:, sl]
            y_ref[:, sl] = o1[2 * L:] + o2[L:]


def _rseq(gf, gb, hf, hb, rf, rb, s_len, c_len):
    b, tt, d = gf.shape
    L = R_CHUNK
    n_lat, n_ctx = s_len // L, c_len // L
    spec = lambda direction: pl.BlockSpec(
        (None, L, d), lambda bi, c: (bi, _seq_block(c, direction, n_lat, n_ctx), 0))
    out = jax.ShapeDtypeStruct((b, tt, d), F32)
    return pl.pallas_call(
        _rseq_body,
        grid=(b, n_lat + n_ctx),
        in_specs=[spec(0), spec(1), spec(0), spec(1), spec(0), spec(1)],
        out_specs=[spec(0), spec(1)],
        out_shape=[out, out],
        scratch_shapes=[pltpu.VMEM((2, L, d), F32)],
        compiler_params=_cp("parallel", "arbitrary"),
        name="rwkv_seq",
    )(gf, gb, hf, hb, rf, rb)


def kernel(x, c, ctx, c_ctx, ada_w, ada_b, ln_g, ln_b, mlp_w1, mlp_w2, m_w_in, m_conv, m_gate_b, m_norm_w, m_w_out, r_w_in, r_mu, r_g2, r_w0, r_w2, r_a0, r_a2, r_k_k, r_k_a, r_r_k, r_lnx_w, r_lnx_b, r_w_out):
    batch, s_len, d = x.shape
    c_len = ctx.shape[1]
    assert batch < MOD_ROWS and s_len % TM == 0 and c_len % M_CHUNK == 0 and c_len <= TM
    assert s_len % GRID_W == 0 and d % R_GROUP == 0 and ada_w.shape[0] == DEPTH == 2
    tt = s_len + TM
    ctx_tile = s_len // TM
    row = lambda a: a.reshape(1, -1)

    h = jnp.concatenate([x, ctx, jnp.zeros((batch, TM - c_len, d), F32)], axis=1)
    cvec = jnp.concatenate([c, c_ctx[None], jnp.zeros((MOD_ROWS - batch - 1, d), F32)], axis=0)
    mod = _ada(cvec, ada_w, ada_b).reshape(DEPTH, MOD_ROWS, 1, 6 * d)

    nqk = 2 * M_HEADS * M_DQK
    nv = M_HEADS * M_DV
    w_in = m_w_in[0]
    p = _proj(h, mod[0], w_in[:, :nqk + 2 * nv].astype(BF16), 1024, batch, ctx_tile, "mlstm_proj")
    w_gates = jnp.pad(w_in[:, nqk + 2 * nv:], ((0, 0), (0, 128 - 4 * M_HEADS))).astype(BF16)
    gates = _proj(h, mod[0], w_gates, 128, batch, ctx_tile, "mlstm_gates")
    gates_t = jnp.swapaxes(gates[:, :, :4 * M_HEADS], 1, 2)
    gb = m_gate_b[0].reshape(-1)
    gb_row = jnp.pad(gb, (0, 128 - gb.shape[0])).reshape(1, 128)
    gb_col = gb.reshape(-1, 1)
    scan = functools.partial(_mlstm_scan, p, gates, gates_t, m_conv[0], gb_row, gb_col)
    h_f = scan(0, s_len, c_len)
    h_b = scan(1, s_len, c_len)
    h = _out_call(_mlstm_out_body, (h_f, h_b, p), (0, 0, (nqk + nv) // d), (row(m_norm_w[0]),),
                  m_w_out[0].astype(BF16), h, mod[0], row(ln_g[0, 0]), row(ln_b[0, 0]),
                  batch, s_len, tt // TMO, "mlstm_out")
    h = _mlp(h, mod[0], mlp_w1[0].astype(BF16), mlp_w2[0].astype(BF16), row(ln_g[0, 1]), row(ln_b[0, 1]),
             batch, ctx_tile, tt // TM, "mlp0")

    w_in = r_w_in[0]
    mu = r_mu[0].reshape(8, 1, d)
    rkv = _rproj(h, mod[1], mu, w_in[:, :3 * d].astype(BF16), d, lambda j: j, batch, s_len, c_len, "rwkv_proj")
    lo_cols = [w_in[:, 3 * d:3 * d + R_GATE_LORA]]
    for gi in range(4):
        start = 3 * d + R_GATE_LORA + gi * R_LORA
        lo_cols.append(jnp.pad(w_in[:, start:start + R_LORA], ((0, 0), (0, R_LORA_PAD - R_LORA))))
    w_lora = jnp.concatenate(lo_cols, axis=1).astype(BF16)
    lora = _rproj(h, mod[1], mu, w_lora, 128, lambda j: 3 + jnp.maximum(j - 1, 0), batch, s_len, c_len,
                  "rwkv_lora")
    pad_rows = ((0, 0), (0, R_LORA_PAD - R_LORA), (0, 0))
    vecs = jnp.concatenate([r_w0[0], r_a0[0], row(r_k_k[0]), row(r_k_a[0]), row(r_r_k[0]),
                            jnp.zeros((1, d), F32)], axis=0)
    rf, rb, y0, bonus, gate, g_f, g_b, ha_f, ha_b = _rloc(
        rkv, lora, r_g2[0].astype(BF16), jnp.pad(r_w2[0], pad_rows).astype(BF16),
        jnp.pad(r_a2[0], pad_rows).astype(BF16), vecs, s_len, c_len)
    y_f, y_b = _rseq(g_f, g_b, ha_f, ha_b, rf, rb, s_len, c_len)
    h1 = _out_call(_rwkv_out_body, (y_f, y_b, y0, bonus, gate), (0, 0, 0, 0, 0),
                   (row(r_lnx_w[0]), row(r_lnx_b[0])), r_w_out[0].astype(BF16), h, mod[1],
                   row(ln_g[1, 0]), row(ln_b[1, 0]), batch, s_len, s_len // TMO, "rwkv_out")
    return _mlp(h1, mod[1], mlp_w1[1].astype(BF16), mlp_w2[1].astype(BF16), row(ln_g[1, 1]), row(ln_b[1, 1]),
                batch, ctx_tile, s_len // TM, "mlp1")
```

```python
import functools

import jax
import jax.numpy as jnp
from jax import lax
from jax.experimental import pallas as pl
from jax.experimental.pallas import tpu as pltpu

F32 = jnp.float32
BF16 = jnp.bfloat16

DEPTH = 2
GRID_W = 64
ALPHA = (2 * DEPTH) ** 0.25
LN_EPS = 1e-5
M_HEADS = 8
M_DQK = 128
M_DV = 256
M_NORM_EPS = 1e-6
R_HEAD = 64
R_GN_EPS = 64e-5
R_LORA = 96
R_LORA_PAD = 128
R_GATE_LORA = 256

TM = 512
TMO = 256
M_CHUNK = 256
R_CHUNK = 64
R_BLOCK = 256
R_GROUP = 256
FF_TILE = 1024
MOD_ROWS = 8
NEG = -1e30
VMEM_LIMIT = 52 * 1024 * 1024


def _cp(*sem):
    return pltpu.CompilerParams(dimension_semantics=sem, vmem_limit_bytes=VMEM_LIMIT)


def _dot(a, b):
    return jnp.dot(a, b, preferred_element_type=F32)


def _dot_nt(a, b):
    return lax.dot_general(a, b, (((1,), (1,)), ((), ())), preferred_element_type=F32)


def _dot_tn(a, b):
    return lax.dot_general(a, b, (((0,), (0,)), ((), ())), preferred_element_type=F32)


def _split2(x):
    hi = x.astype(BF16)
    lo = (x - hi.astype(F32)).astype(BF16)
    return hi, lo


def _split3(x):
    hi = x.astype(BF16)
    r = x - hi.astype(F32)
    mid = r.astype(BF16)
    lo = (r - mid.astype(F32)).astype(BF16)
    return hi, mid, lo


def _sigmoid(x):
    return 1.0 / (1.0 + jnp.exp(-x))


def _log_sigmoid(x):
    return jnp.minimum(x, 0.0) - jnp.log(1.0 + jnp.exp(-jnp.abs(x)))


def _layer_norm(z, g, b):
    mu = jnp.mean(z, axis=-1, keepdims=True)
    zc = z - mu
    var = jnp.mean(zc * zc, axis=-1, keepdims=True)
    return zc * lax.rsqrt(var + LN_EPS) * g + b


def _iota(shape, dim):
    return lax.broadcasted_iota(jnp.int32, shape, dim)


def _group_ones(n, group):
    return jnp.where(_iota((n, n), 0) // group == _iota((n, n), 1) // group, 1.0, 0.0).astype(BF16)


def _ada_body(c_ref, w_ref, b_ref, o_ref):
    x = c_ref[...]
    x = x * _sigmoid(x)
    xh, xl = _split2(x)
    wh, wl = _split2(w_ref[...])
    o_ref[...] = _dot(xh, wh) + _dot(xh, wl) + _dot(xl, wh) + b_ref[...]


def _ada(cvec, ada_w, ada_b):
    depth, d, n = ada_w.shape
    tn = 1024
    return pl.pallas_call(
        _ada_body,
        grid=(depth, n // tn),
        in_specs=[pl.BlockSpec((MOD_ROWS, d), lambda l, j: (0, 0)),
                  pl.BlockSpec((None, d, tn), lambda l, j: (l, 0, j)),
                  pl.BlockSpec((None, 1, tn), lambda l, j: (l, 0, j))],
        out_specs=pl.BlockSpec((None, MOD_ROWS, tn), lambda l, j: (l, 0, j)),
        out_shape=jax.ShapeDtypeStruct((depth, MOD_ROWS, n), F32),
        compiler_params=_cp("parallel", "parallel"),
        name="ada_mod",
    )(cvec, ada_w, ada_b.reshape(depth, 1, n))


def _mod_spec(d, which, row_fn):
    return pl.BlockSpec((None, 1, d), lambda *g: (row_fn(*g), 0, which))


def _proj_body(h_ref, sc_ref, sh_ref, w_ref, o_ref, u_ref):
    @pl.when(pl.program_id(2) == 0)
    def _():
        u_ref[...] = (h_ref[...] * (1.0 + sc_ref[...]) + sh_ref[...]).astype(BF16)

    o_ref[...] = _dot(u_ref[...], w_ref[...])


def _proj(h, mod, w, tn, batch, ctx_tile, name):
    b, tt, d = h.shape
    n = w.shape[1]
    row = lambda bi, i, j: jnp.where(i == ctx_tile, batch, bi)
    return pl.pallas_call(
        _proj_body,
        grid=(b, tt // TM, n // tn),
        in_specs=[pl.BlockSpec((None, TM, d), lambda bi, i, j: (bi, i, 0)),
                  _mod_spec(d, 1, row), _mod_spec(d, 0, row),
                  pl.BlockSpec((d, tn), lambda bi, i, j: (0, j))],
        out_specs=pl.BlockSpec((None, TM, tn), lambda bi, i, j: (bi, i, j)),
        out_shape=jax.ShapeDtypeStruct((b, tt, n), F32),
        scratch_shapes=[pltpu.VMEM((TM, d), BF16)],
        compiler_params=_cp("parallel", "parallel", "arbitrary"),
        name=name,
    )(h, mod, mod, w)


def _seq_block(c, direction, n_lat, n_ctx):
    if direction == 0:
        return jnp.where(c < n_ctx, n_lat + c, c - n_ctx)
    return jnp.where(c < n_ctx, n_lat + n_ctx - 1 - c, n_lat - 1 - (c - n_ctx))


def _mlstm_body(direction, n_lat, n_ctx,
                qk_ref, qkp_ref, qkn_ref, v_ref, g_ref, gt_ref, cw_ref, gb_ref, gbt_ref,
                o_ref, ct_ref, n_ref, m_ref):
    L = M_CHUNK
    c = pl.program_id(1)
    n_live = n_lat + n_ctx

    @pl.when(c == 0)
    def _():
        ct_ref[...] = jnp.zeros_like(ct_ref)
        n_ref[...] = jnp.zeros_like(n_ref)
        m_ref[...] = jnp.zeros_like(m_ref)

    @pl.when(c >= n_live)
    def _():
        o_ref[...] = jnp.zeros_like(o_ref)

    @pl.when(c < n_live)
    def _():
        blk = _seq_block(c, direction, n_lat, n_ctx)
        first = jnp.logical_or(blk == 0, blk == n_lat)
        last = jnp.logical_or(blk == n_lat - 1, blk == n_live - 1)
        p = qk_ref[...]
        rows = _iota((L, 1), 0)
        prev_row = jnp.where(first, 0.0, qkp_ref[7:8, :])
        next_row = jnp.where(last, 0.0, qkn_ref[0:1, :])
        pm = jnp.where(rows == 0, prev_row, pltpu.roll(p, 1, 0))
        pp = jnp.where(rows == L - 1, next_row, pltpu.roll(p, L - 1, 0))
        qkc = cw_ref[0:1, :] * pm + cw_ref[1:2, :] * p + cw_ref[2:3, :] * pp
        qk = qkc * _sigmoid(qkc)
        nq = M_HEADS * M_DQK
        q_all = qk[:, :nq]
        k_all = qk[:, nq:] * (M_DQK ** -0.5)
        v_all = v_ref[...]

        g = g_ref[...] + gb_ref[...]
        gt = gt_ref[...] + gbt_ref[...]
        r_i = _iota((L, L), 0)
        c_i = _iota((L, L), 1)
        causal = (r_i >= c_i) if direction == 0 else (r_i <= c_i)
        tri = jnp.where(causal, 1.0, 0.0).astype(BF16)
        tri_t = jnp.where(causal, 0.0, 1.0).astype(BF16) + jnp.where(r_i == c_i, 1.0, 0.0).astype(BF16)
        lf = _log_sigmoid(g)
        lft = _log_sigmoid(gt)
        b_col = sum(_dot(tri, part) for part in _split3(lf))
        b_row = sum(_dot(part, tri_t) for part in _split3(lft))
        e_last = L - 1 if direction == 0 else 0

        for hd in range(M_HEADS):
            ci = 16 * direction + hd
            cf = ci + 8
            b_c = b_col[:, cf:cf + 1]
            i_c = g[:, ci:ci + 1]
            b_r = b_row[cf:cf + 1, :]
            i_r = gt[ci:ci + 1, :]
            m_st = m_ref[hd][0:1, 0:1]
            n_st = n_ref[hd][0:1, :]
            q_h = q_all[:, hd * M_DQK:(hd + 1) * M_DQK]
            k_h = k_all[:, hd * M_DQK:(hd + 1) * M_DQK]
            v_h = v_all[:, hd * M_DV:(hd + 1) * M_DV]
            q_b = q_h.astype(BF16)
            k_b = k_h.astype(BF16)

            dmat = jnp.where(causal, b_c - b_r + i_r, NEG)
            inter = b_c + m_st
            m_t = jnp.maximum(inter, jnp.max(dmat, axis=-1, keepdims=True))
            s = _dot_nt(q_b, k_b) * jnp.exp(dmat - m_t)
            w_inter = jnp.exp(inter - m_t)
            num = _dot(s.astype(BF16), v_h.astype(BF16)) + w_inter * _dot(q_b, ct_ref[hd].astype(BF16))
            den = jnp.sum(s, axis=-1, keepdims=True) + w_inter * jnp.sum(q_h * n_st, axis=-1, keepdims=True)
            o_ref[:, hd * M_DV:(hd + 1) * M_DV] = num / jnp.maximum(jnp.abs(den), jnp.exp(-m_t))

            b_last = b_c[e_last:e_last + 1, :]
            g_c = b_last - b_c + i_c
            g_r = b_last - b_r + i_r
            m_new = jnp.maximum(b_last + m_st, jnp.max(g_r, axis=-1, keepdims=True))
            decay = jnp.exp(b_last + m_st - m_new)
            wg_c = jnp.exp(g_c - m_new)
            wg_r = jnp.exp(g_r - m_new)
            ct_ref[hd] = decay * ct_ref[hd] + _dot_tn(k_b, (wg_c * v_h).astype(BF16))
            n_new = decay * n_st + _dot(jnp.broadcast_to(wg_r, (8, L)).astype(BF16), k_b)[0:1, :]
            n_ref[hd] = jnp.broadcast_to(n_new, (8, M_DQK))
            m_ref[hd] = jnp.broadcast_to(m_new, (8, 128))


def _mlstm_scan(p, gates, gates_t, conv_w, gb_row, gb_col, direction, s_len, c_len):
    b, tt, _ = p.shape
    L = M_CHUNK
    n_lat, n_ctx = s_len // L, c_len // L
    n_all = tt // L
    nqk = 2 * M_HEADS * M_DQK
    nv = M_HEADS * M_DV

    def blk(c):
        return jnp.where(c < n_lat + n_ctx, _seq_block(c, direction, n_lat, n_ctx), c)

    r8 = L // 8
    return pl.pallas_call(
        functools.partial(_mlstm_body, direction, n_lat, n_ctx),
        grid=(b, n_all),
        in_specs=[pl.BlockSpec((None, L, nqk), lambda bi, c: (bi, blk(c), 0)),
                  pl.BlockSpec((None, 8, nqk), lambda bi, c: (bi, jnp.maximum(blk(c) * r8 - 1, 0), 0)),
                  pl.BlockSpec((None, 8, nqk), lambda bi, c: (bi, jnp.minimum((blk(c) + 1) * r8, tt // 8 - 1), 0)),
                  pl.BlockSpec((None, L, nv), lambda bi, c: (bi, blk(c), 1)),
                  pl.BlockSpec((None, L, 128), lambda bi, c: (bi, blk(c), 0)),
                  pl.BlockSpec((None, 32, L), lambda bi, c: (bi, 0, blk(c))),
                  pl.BlockSpec((3, nqk), lambda bi, c: (0, 0)),
                  pl.BlockSpec((1, 128), lambda bi, c: (0, 0)),
                  pl.BlockSpec((32, 1), lambda bi, c: (0, 0))],
        out_specs=pl.BlockSpec((None, L, nv), lambda bi, c: (bi, blk(c), 0)),
        out_shape=jax.ShapeDtypeStruct((b, tt, nv), F32),
        scratch_shapes=[pltpu.VMEM((M_HEADS, M_DQK, M_DV), F32),
                        pltpu.VMEM((M_HEADS, 8, M_DQK), F32),
                        pltpu.VMEM((M_HEADS, 8, 128), F32)],
        compiler_params=_cp("parallel", "arbitrary"),
        name="mlstm_scan_%d" % direction,
    )(p, p, p, p, gates, gates_t, conv_w, gb_row, gb_col)


def _out_epilogue(z_b, w_ref, res_ref, gate_ref, lg_ref, lb_ref, o_ref):
    y = _dot(z_b, w_ref[...])
    o_ref[...] = _layer_norm(ALPHA * res_ref[...] + gate_ref[...] * y, lg_ref[...], lb_ref[...])


def _mlstm_out_body(hf_ref, hb_ref, og_ref, nw_ref, w_ref, res_ref, gate_ref, lg_ref, lb_ref, o_ref, z_ref):
    ones = jnp.full((M_DV, M_DV), 1.0, BF16)
    for hd in range(M_HEADS):
        sl = slice(hd * M_DV, (hd + 1) * M_DV)
        hh = hf_ref[:, sl] + hb_ref[:, sl]
        mu = _dot(hh.astype(BF16), ones) * (1.0 / M_DV)
        hc = hh - mu
        var = _dot((hc * hc).astype(BF16), ones) * (1.0 / M_DV)
        hn = hc * lax.rsqrt(var + M_NORM_EPS) * nw_ref[:, sl]
        z_ref[:, sl] = (_sigmoid(og_ref[:, sl]) * hn).astype(BF16)
    _out_epilogue(z_ref[...], w_ref, res_ref, gate_ref, lg_ref, lb_ref, o_ref)


def _rwkv_out_body(yf_ref, yb_ref, y0_ref, bon_ref, gt_ref, lw_ref, lbx_ref,
                   w_ref, res_ref, gate_ref, lg_ref, lb_ref, o_ref, z_ref):
    ones = _group_ones(R_GROUP, R_HEAD)
    d = o_ref.shape[-1]
    for gi in range(d // R_GROUP):
        sl = slice(gi * R_GROUP, (gi + 1) * R_GROUP)
        y = yf_ref[:, sl] + yb_ref[:, sl] + y0_ref[:, sl]
        mu = _dot(y.astype(BF16), ones) * (1.0 / R_HEAD)
        yc = y - mu
        var = _dot((yc * yc).astype(BF16), ones) * (1.0 / R_HEAD)
        yn = yc * lax.rsqrt(var + R_GN_EPS) * lw_ref[:, sl] + lbx_ref[:, sl]
        z_ref[:, sl] = ((yn + bon_ref[:, sl]) * gt_ref[:, sl]).astype(BF16)
    _out_epilogue(z_ref[...], w_ref, res_ref, gate_ref, lg_ref, lb_ref, o_ref)


def _out_call(body, streams, stream_col, vecs, w, res, mod, ln_g, ln_b, batch, s_len, n_tiles, name):
    b, _, d = res.shape
    row = lambda bi, i: jnp.where(i * TMO >= s_len, batch, bi)
    tile = lambda col: pl.BlockSpec((None, TMO, d), lambda bi, i: (bi, i, col))
    vec = pl.BlockSpec((1, d), lambda bi, i: (0, 0))
    return pl.pallas_call(
        body,
        grid=(b, n_tiles),
        in_specs=[tile(col) for col in stream_col] + [vec] * len(vecs)
                 + [pl.BlockSpec((d, d), lambda bi, i: (0, 0)), tile(0), _mod_spec(d, 2, row), vec, vec],
        out_specs=tile(0),
        out_shape=jax.ShapeDtypeStruct((b, n_tiles * TMO, d), F32),
        scratch_shapes=[pltpu.VMEM((TMO, d), BF16)],
        compiler_params=_cp("parallel", "parallel"),
        name=name,
    )(*streams, *vecs, w, res, mod, ln_g, ln_b)


def _mlp_body(h_ref, sc_ref, sh_ref, gate_ref, w1_ref, w2_ref, lg_ref, lb_ref, o_ref, u_ref, acc_ref):
    k = pl.program_id(2)

    @pl.when(k == 0)
    def _():
        u_ref[...] = (h_ref[...] * (1.0 + sc_ref[...]) + sh_ref[...]).astype(BF16)
        acc_ref[...] = jnp.zeros_like(acc_ref)

    a = jnp.maximum(_dot(u_ref[...], w1_ref[...]), 0.0)
    acc_ref[...] += _dot((a * a).astype(BF16), w2_ref[...])

    @pl.when(k == pl.num_programs(2) - 1)
    def _():
        o_ref[...] = _layer_norm(ALPHA * h_ref[...] + gate_ref[...] * acc_ref[...], lg_ref[...], lb_ref[...])


def _mlp(h, mod, w1, w2, ln_g, ln_b, batch, ctx_tile, n_tiles, name):
    b, _, d = h.shape
    dff = w1.shape[1]
    row = lambda bi, i, k: jnp.where(i == ctx_tile, batch, bi)
    tile = pl.BlockSpec((None, TM, d), lambda bi, i, k: (bi, i, 0))
    vec = pl.BlockSpec((1, d), lambda bi, i, k: (0, 0))
    return pl.pallas_call(
        _mlp_body,
        grid=(b, n_tiles, dff // FF_TILE),
        in_specs=[tile, _mod_spec(d, 4, row), _mod_spec(d, 3, row), _mod_spec(d, 5, row),
                  pl.BlockSpec((d, FF_TILE), lambda bi, i, k: (0, k)),
                  pl.BlockSpec((FF_TILE, d), lambda bi, i, k: (k, 0)), vec, vec],
        out_specs=tile,
        out_shape=jax.ShapeDtypeStruct((b, n_tiles * TM, d), F32),
        scratch_shapes=[pltpu.VMEM((TM, d), BF16), pltpu.VMEM((TM, d), F32)],
        compiler_params=_cp("parallel", "parallel", "arbitrary"),
        name=name,
    )(h, mod, mod, mod, w1, w2, ln_g, ln_b)


def _rproj_body(ctx_tile, c_len, n_lat_tiles,
                h_ref, hp_ref, hn_ref, sc_ref, sh_ref, mu_ref, w_ref, o_ref, u_ref, xx_ref):
    i = pl.program_id(1)

    @pl.when(pl.program_id(2) == 0)
    def _():
        is_ctx = i == ctx_tile
        sc = 1.0 + sc_ref[...]
        sh = sh_ref[...]
        u = h_ref[...] * sc + sh
        u_prev = hp_ref[...] * sc + sh
        u_next = hn_ref[...] * sc + sh
        u_ref[...] = u
        t = _iota((TM, 1), 0)
        tw = jnp.bitwise_and(t, GRID_W - 1)
        q = u.shape[1] // 4
        flag = lambda cond: jnp.where(cond, 1, 0)
        ctx_prev, ctx_next = flag(t != 0), flag(t != c_len - 1)
        lat_left, lat_right = flag(tw != 0), flag(tw != GRID_W - 1)
        lat_up = jnp.where(i == 0, flag(t >= GRID_W), 1)
        lat_down = jnp.where(i == n_lat_tiles - 1, flag(t < TM - GRID_W), 1)
        for qi in range(4):
            sl = slice(qi * q, (qi + 1) * q)
            uq = u[:, sl]
            before = pltpu.roll(uq, 1, 0)
            after = pltpu.roll(uq, TM - 1, 0)
            if qi == 0:
                src = before
                keep = jnp.where(is_ctx, ctx_prev, lat_left)
            elif qi == 1:
                src = jnp.where(is_ctx, before, after)
                keep = jnp.where(is_ctx, ctx_prev, lat_right)
            elif qi == 2:
                above = jnp.concatenate([u_prev[:, sl], uq[:TM - GRID_W]], axis=0)
                src = jnp.where(is_ctx, after, above)
                keep = jnp.where(is_ctx, ctx_next, lat_up)
            else:
                below = jnp.concatenate([uq[GRID_W:], u_next[:, sl]], axis=0)
                src = jnp.where(is_ctx, after, below)
                keep = jnp.where(is_ctx, ctx_next, lat_down)
            xx_ref[:, sl] = jnp.where(keep != 0, src, 0.0) - uq

    x = (u_ref[...] + xx_ref[...] * mu_ref[...]).astype(BF16)
    o_ref[...] = _dot(x, w_ref[...])


def _rproj(h, mod, mu, w, tn, mu_of_tile, batch, s_len, c_len, name):
    b, tt, d = h.shape
    n = w.shape[1]
    ctx_tile = s_len // TM
    row = lambda bi, i, j: jnp.where(i == ctx_tile, batch, bi)
    r64 = TM // GRID_W
    return pl.pallas_call(
        functools.partial(_rproj_body, ctx_tile, c_len, s_len // TM),
        grid=(b, tt // TM, n // tn),
        in_specs=[pl.BlockSpec((None, TM, d), lambda bi, i, j: (bi, i, 0)),
                  pl.BlockSpec((None, GRID_W, d), lambda bi, i, j: (bi, jnp.maximum(i * r64 - 1, 0), 0)),
                  pl.BlockSpec((None, GRID_W, d),
                               lambda bi, i, j: (bi, jnp.minimum((i + 1) * r64, tt // GRID_W - 1), 0)),
                  _mod_spec(d, 1, row), _mod_spec(d, 0, row),
                  pl.BlockSpec((None, 1, d), lambda bi, i, j: (mu_of_tile(j), 0, 0)),
                  pl.BlockSpec((d, tn), lambda bi, i, j: (0, j))],
        out_specs=pl.BlockSpec((None, TM, tn), lambda bi, i, j: (bi, i, j)),
        out_shape=jax.ShapeDtypeStruct((b, tt, n), F32),
        scratch_shapes=[pltpu.VMEM((TM, d), F32), pltpu.VMEM((TM, d), F32)],
        compiler_params=_cp("parallel", "parallel", "arbitrary"),
        name=name,
    )(h, h, h, mod, mod, mu, w)


def _block_diag(x, mask):
    return jnp.where(mask, jnp.concatenate([x] * (R_GROUP // R_CHUNK), axis=0), 0.0)


def _rloc_body(r_ref, k_ref, v_ref, lo_ref, g2_ref, w2_ref, a2_ref, vec_ref,
               rf_ref, rb_ref, y0_ref, bon_ref, gate_ref, gf_ref, gb_ref, hf_ref, hb_ref,
               rs_ref, vs_ref, kk_ref, lw_ref, kd_ref, bb_ref):
    L = R_CHUNK
    W = R_GROUP
    lo = lo_ref[...]
    r = r_ref[...]
    k = k_ref[...]
    v = v_ref[...]
    w0 = (vec_ref[0:1, :], vec_ref[1:2, :])
    a0 = (vec_ref[2:3, :], vec_ref[3:4, :])
    k_k = vec_ref[4:5, :]
    k_a = vec_ref[5:6, :]
    r_k = vec_ref[6:7, :]
    ones = _group_ones(W, R_HEAD)

    gate_ref[...] = _dot(_sigmoid(lo[:, :R_GATE_LORA]).astype(BF16), g2_ref[...])
    kraw = k * k_k
    ss = _dot((kraw * kraw).astype(BF16), ones)
    kk = kraw / jnp.maximum(jnp.sqrt(ss), 1e-12)
    rs_ref[...] = r
    vs_ref[...] = v
    kk_ref[...] = kk
    kd_sum = jnp.zeros_like(k)
    for d in range(2):
        wl = lo[:, R_GATE_LORA + d * R_LORA_PAD:R_GATE_LORA + (d + 1) * R_LORA_PAD]
        al = lo[:, R_GATE_LORA + (2 + d) * R_LORA_PAD:R_GATE_LORA + (3 + d) * R_LORA_PAD]
        lwx = w0[d] + _dot(jnp.tanh(wl).astype(BF16), w2_ref[d])
        lw_ref[d] = -_sigmoid(lwx) * (jnp.e ** -0.5)
        iclr = _sigmoid(a0[d] + _dot(al.astype(BF16), a2_ref[d]))
        kd = k * (1.0 + (iclr - 1.0) * k_a)
        kd_ref[d] = kd
        bb_ref[d] = kk * iclr
        kd_sum = kd_sum + kd
    bon_ref[...] = _dot((r * r_k * kd_sum).astype(BF16), ones) * v

    lane_head = _iota((W, W), 1) // R_HEAD
    bd_mask = _iota((W, W), 0) // L == lane_head
    t_i = _iota((L, W), 0)
    s_i = jnp.bitwise_and(_iota((L, W), 1), L - 1)
    eye = jnp.where(t_i == s_i, 1.0, 0.0)
    ts_xor = jnp.bitwise_xor(t_i, s_i)
    diag2 = jnp.right_shift(ts_xor, 1) == 0
    off_masks = [jnp.right_shift(ts_xor, lv) == 1 for lv in range(1, 6)]
    head_of_lane = _iota((L, W), 1) // R_HEAD
    r_i = _iota((L, L), 0)
    c_i = _iota((L, L), 1)
    outs = ((rf_ref, gf_ref, hf_ref), (rb_ref, gb_ref, hb_ref))

    tris = (jnp.where(r_i >= c_i, 1.0, 0.0).astype(BF16), jnp.where(r_i <= c_i, 1.0, 0.0).astype(BF16))
    stricts = (t_i > s_i, t_i < s_i)
    incls = (t_i >= s_i, t_i <= s_i)
    bd = lambda x: _block_diag(x, bd_mask).astype(BF16)

    n_chunks = R_BLOCK // L
    inst = [(ci, d) for ci in range(n_chunks) for d in range(2)]
    rows_of = [pl.ds(ci * L, L) for ci in range(n_chunks)]
    vcs = [vs_ref[rows_of[ci], :] for ci in range(n_chunks)]
    bd_vs = [bd(vc) for vc in vcs]

    st = []
    for ci, d in inst:
        rows = rows_of[ci]
        lwc = lw_ref[d, rows, :]
        cum = sum(_dot(tris[d], part) for part in _split3(lwc))
        e_last = L - 1 if d == 0 else 0
        tot = cum[e_last:e_last + 1, :]
        p_inv = jnp.exp(-cum)
        p_end = jnp.exp(tot - cum)
        bc = bb_ref[d, rows, :]
        kdc = kd_ref[d, rows, :]
        st.append(dict(tot=tot, a_t=-kk_ref[rows, :] * jnp.exp(cum - lwc), r_t=rs_ref[rows, :] * jnp.exp(cum),
                       b_inv=bc * p_inv, k_inv=kdc * p_inv,
                       b_end=(bc * p_end).astype(BF16), k_end=(kdc * p_end).astype(BF16)))
    for s, (ci, d) in zip(st, inst):
        z = _dot_nt(jnp.concatenate([s["a_t"], s["r_t"]], axis=0).astype(BF16),
                    jnp.concatenate([bd(s.pop("b_inv")), bd(s.pop("k_inv"))], axis=0))
        s["a_ab"] = jnp.where(stricts[d], z[:L, :W], 0.0)
        s["a_ak"] = jnp.where(stricts[d], z[:L, W:], 0.0).astype(BF16)
        s["m_rb"] = jnp.where(incls[d], z[L:, :W], 0.0).astype(BF16)
        s["m_rk"] = jnp.where(incls[d], z[L:, W:], 0.0).astype(BF16)
        s["tinv"] = eye + jnp.where(diag2, s["a_ab"], 0.0)
    for s, (ci, d) in zip(st, inst):
        s["akv"] = _dot(s.pop("a_ak"), bd_vs[ci])
    for off in off_masks:
        for s in st:
            s["x1"] = _dot(s["tinv"].astype(BF16), bd(jnp.where(off, s["a_ab"], 0.0)))
        for s in st:
            s["tinv"] = s["tinv"] + _dot(s.pop("x1").astype(BF16), bd(s["tinv"]))
    for s in st:
        x = _dot(s.pop("tinv").astype(BF16), jnp.concatenate([bd(s.pop("a_t")), bd(s.pop("akv"))], axis=1))
        s["a_p"] = x[:, :W]
        s["u0"] = x[:, W:]
    y0s = [jnp.zeros((L, W), F32) for _ in range(n_chunks)]
    for s, (ci, d) in zip(st, inst):
        y2 = _dot(s.pop("m_rb"), jnp.concatenate([bd(s["a_p"]), bd(s["u0"])], axis=1))
        outs[d][0][rows_of[ci], :] = (s.pop("r_t") + y2[:, :W]).astype(BF16)
        y0s[ci] = y0s[ci] + y2[:, W:] + _dot(s.pop("m_rk"), bd_vs[ci])
    for ci in range(n_chunks):
        y0_ref[rows_of[ci], :] = y0s[ci]
    for s, (ci, d) in zip(st, inst):
        zs = _dot_tn(s.pop("b_end"), jnp.concatenate([s.pop("a_p"), s.pop("u0")], axis=1).astype(BF16))
        zk = _dot_tn(s.pop("k_end"), vcs[ci].astype(BF16))
        g_full = zs[:, :W]
        h_full = zs[:, W:] + zk
        g_sbs = eye * jnp.exp(s["tot"])
        h_sbs = jnp.zeros((L, W), F32)
        for hd in range(W // R_HEAD):
            sel = head_of_lane == hd
            g_sbs = g_sbs + jnp.where(sel, g_full[hd * R_HEAD:(hd + 1) * R_HEAD, :], 0.0)
            h_sbs = h_sbs + jnp.where(sel, h_full[hd * R_HEAD:(hd + 1) * R_HEAD, :], 0.0)
        outs[d][1][rows_of[ci], :] = g_sbs
        outs[d][2][rows_of[ci], :] = h_sbs


def _rloc(rkv, lora, g2, w2p, a2p, vecs, s_len, c_len):
    b, tt, n3 = rkv.shape
    d = n3 // 3
    nb = (s_len + c_len) // R_BLOCK
    ng = d // R_GROUP
    tok = lambda col: pl.BlockSpec((None, R_BLOCK, R_GROUP), lambda bi, i, j: (bi, i, col * ng + j))
    out_tok = pl.BlockSpec((None, R_BLOCK, R_GROUP), lambda bi, i, j: (bi, i, j))
    f32_out = jax.ShapeDtypeStruct((b, tt, d), F32)
    bf_out = jax.ShapeDtypeStruct((b, tt, d), BF16)
    sc = lambda *shape: pltpu.VMEM(shape, F32)
    return pl.pallas_call(
        _rloc_body,
        grid=(b, nb, ng),
        in_specs=[tok(0), tok(1), tok(2),
                  pl.BlockSpec((None, R_BLOCK, lora.shape[2]), lambda bi, i, j: (bi, i, 0)),
                  pl.BlockSpec((R_GATE_LORA, R_GROUP), lambda bi, i, j: (0, j)),
                  pl.BlockSpec((2, R_LORA_PAD, R_GROUP), lambda bi, i, j: (0, 0, j)),
                  pl.BlockSpec((2, R_LORA_PAD, R_GROUP), lambda bi, i, j: (0, 0, j)),
                  pl.BlockSpec((8, R_GROUP), lambda bi, i, j: (0, j))],
        out_specs=[out_tok] * 9,
        out_shape=[bf_out, bf_out, f32_out, f32_out, f32_out, f32_out, f32_out, f32_out, f32_out],
        scratch_shapes=[sc(R_BLOCK, R_GROUP), sc(R_BLOCK, R_GROUP), sc(R_BLOCK, R_GROUP),
                        sc(2, R_BLOCK, R_GROUP), sc(2, R_BLOCK, R_GROUP), sc(2, R_BLOCK, R_GROUP)],
        compiler_params=_cp("parallel", "parallel", "arbitrary"),
        name="rwkv_local",
    )(rkv, rkv, rkv, lora, g2, w2p, a2p, vecs)


def _rseq_body(gf_ref, gb_ref, hf_ref, hb_ref, rf_ref, rb_ref, yf_ref, yb_ref, st_ref):
    L = R_CHUNK
    W = R_GROUP

    @pl.when(pl.program_id(1) == 0)
    def _():
        st_ref[...] = jnp.zeros_like(st_ref)

    bd_mask = _iota((W, W), 0) // L == _iota((W, W), 1) // R_HEAD
    ins = ((gf_ref, hf_ref, rf_ref, yf_ref), (gb_ref, hb_ref, rb_ref, yb_ref))
    for d in range(2):
        g_ref, ha_ref, rp_ref, y_ref = ins[d]
        for gi in range(st_ref.shape[-1] // W):
            sl = slice(gi * W, (gi + 1) * W)
            st = st_ref[d, :, sl]
            st_hi = st.astype(BF16)
            st_lo = (st - st_hi.astype(F32)).astype(BF16)
            g_hi, g_lo = _split2(g_ref[:, sl])
            rp = rp_ref[:, sl]
            bd_hi = _block_diag(st_hi.astype(F32), bd_mask).astype(BF16)
            bd_lo = _block_diag(st_lo.astype(F32), bd_mask).astype(BF16)
            o1 = _dot(jnp.concatenate([g_hi, g_lo, rp], axis=0), bd_hi)
            o2 = _dot(jnp.concatenate([g_hi, rp], axis=0), bd_lo)
            st_ref[d, :, sl] = o1[:L] + o1[L:2 * L] + o2[:L] + ha_ref[:, sl]
            y_ref[:, sl] = o1[2 * L:] + o2[L:]


def _rseq(gf, gb, hf, hb, rf, rb, s_len, c_len):
    b, tt, d = gf.shape
    L = R_CHUNK
    n_lat, n_ctx = s_len // L, c_len // L
    spec = lambda direction: pl.BlockSpec(
        (None, L, d), lambda bi, c: (bi, _seq_block(c, direction, n_lat, n_ctx), 0))
    out = jax.ShapeDtypeStruct((b, tt, d), F32)
    return pl.pallas_call(
        _rseq_body,
        grid=(b, n_lat + n_ctx),
        in_specs=[spec(0), spec(1), spec(0), spec(1), spec(0), spec(1)],
        out_specs=[spec(0), spec(1)],
        out_shape=[out, out],
        scratch_shapes=[pltpu.VMEM((2, L, d), F32)],
        compiler_params=_cp("parallel", "arbitrary"),
        name="rwkv_seq",
    )(gf, gb, hf, hb, rf, rb)


def kernel(x, c, ctx, c_ctx, ada_w, ada_b, ln_g, ln_b, mlp_w1, mlp_w2, m_w_in, m_conv, m_gate_b, m_norm_w, m_w_out, r_w_in, r_mu, r_g2, r_w0, r_w2, r_a0, r_a2, r_k_k, r_k_a, r_r_k, r_lnx_w, r_lnx_b, r_w_out):
    batch, s_len, d = x.shape
    c_len = ctx.shape[1]
    assert batch < MOD_ROWS and s_len % TM == 0 and c_len % M_CHUNK == 0 and c_len <= TM
    assert s_len % GRID_W == 0 and d % R_GROUP == 0 and ada_w.shape[0] == DEPTH == 2
    tt = s_len + TM
    ctx_tile = s_len // TM
    row = lambda a: a.reshape(1, -1)

    h = jnp.concatenate([x, ctx, jnp.zeros((batch, TM - c_len, d), F32)], axis=1)
    cvec = jnp.concatenate([c, c_ctx[None], jnp.zeros((MOD_ROWS - batch - 1, d), F32)], axis=0)
    mod = _ada(cvec, ada_w, ada_b).reshape(DEPTH, MOD_ROWS, 1, 6 * d)

    nqk = 2 * M_HEADS * M_DQK
    nv = M_HEADS * M_DV
    w_in = m_w_in[0]
    p = _proj(h, mod[0], w_in[:, :nqk + 2 * nv].astype(BF16), 1024, batch, ctx_tile, "mlstm_proj")
    w_gates = jnp.pad(w_in[:, nqk + 2 * nv:], ((0, 0), (0, 128 - 4 * M_HEADS))).astype(BF16)
    gates = _proj(h, mod[0], w_gates, 128, batch, ctx_tile, "mlstm_gates")
    gates_t = jnp.swapaxes(gates[:, :, :4 * M_HEADS], 1, 2)
    gb = m_gate_b[0].reshape(-1)
    gb_row = jnp.pad(gb, (0, 128 - gb.shape[0])).reshape(1, 128)
    gb_col = gb.reshape(-1, 1)
    scan = functools.partial(_mlstm_scan, p, gates, gates_t, m_conv[0], gb_row, gb_col)
    h_f = scan(0, s_len, c_len)
    h_b = scan(1, s_len, c_len)
    h = _out_call(_mlstm_out_body, (h_f, h_b, p), (0, 0, (nqk + nv) // d), (row(m_norm_w[0]),),
                  m_w_out[0].astype(BF16), h, mod[0], row(ln_g[0, 0]), row(ln_b[0, 0]),
                  batch, s_len, tt // TMO, "mlstm_out")
    h = _mlp(h, mod[0], mlp_w1[0].astype(BF16), mlp_w2[0].astype(BF16), row(ln_g[0, 1]), row(ln_b[0, 1]),
             batch, ctx_tile, tt // TM, "mlp0")

    w_in = r_w_in[0]
    mu = r_mu[0].reshape(8, 1, d)
    rkv = _rproj(h, mod[1], mu, w_in[:, :3 * d].astype(BF16), d, lambda j: j, batch, s_len, c_len, "rwkv_proj")
    lo_cols = [w_in[:, 3 * d:3 * d + R_GATE_LORA]]
    for gi in range(4):
        start = 3 * d + R_GATE_LORA + gi * R_LORA
        lo_cols.append(jnp.pad(w_in[:, start:start + R_LORA], ((0, 0), (0, R_LORA_PAD - R_LORA))))
    w_lora = jnp.concatenate(lo_cols, axis=1).astype(BF16)
    lora = _rproj(h, mod[1], mu, w_lora, 128, lambda j: 3 + jnp.maximum(j - 1, 0), batch, s_len, c_len,
                  "rwkv_lora")
    pad_rows = ((0, 0), (0, R_LORA_PAD - R_LORA), (0, 0))
    vecs = jnp.concatenate([r_w0[0], r_a0[0], row(r_k_k[0]), row(r_k_a[0]), row(r_r_k[0]),
                            jnp.zeros((1, d), F32)], axis=0)
    rf, rb, y0, bonus, gate, g_f, g_b, ha_f, ha_b = _rloc(
        rkv, lora, r_g2[0].astype(BF16), jnp.pad(r_w2[0], pad_rows).astype(BF16),
        jnp.pad(r_a2[0], pad_rows).astype(BF16), vecs, s_len, c_len)
    y_f, y_b = _rseq(g_f, g_b, ha_f, ha_b, rf, rb, s_len, c_len)
    h1 = _out_call(_rwkv_out_body, (y_f, y_b, y0, bonus, gate), (0, 0, 0, 0, 0),
                   (row(r_lnx_w[0]), row(r_lnx_b[0])), r_w_out[0].astype(BF16), h, mod[1],
                   row(ln_g[1, 0]), row(ln_b[1, 0]), batch, s_len, s_len // TMO, "rwkv_out")
    return _mlp(h1, mod[1], mlp_w1[1].astype(BF16), mlp_w2[1].astype(BF16), row(ln_g[1, 1]), row(ln_b[1, 1]),
                batch, ctx_tile, s_len // TM, "mlp1")
```

```python
import functools

import jax
import jax.numpy as jnp
from jax import lax
from jax.experimental import pallas as pl
from jax.experimental.pallas import tpu as pltpu

F32 = jnp.float32
BF16 = jnp.bfloat16

DEPTH = 2
GRID_W = 64
ALPHA = (2 * DEPTH) ** 0.25
LN_EPS = 1e-5
M_HEADS = 8
M_DQK = 128
M_DV = 256
M_NORM_EPS = 1e-6
R_HEAD = 64
R_GN_EPS = 64e-5
R_LORA = 96
R_LORA_PAD = 128
R_GATE_LORA = 256

TM = 512
TMO = 256
M_CHUNK = 256
HALO = 16
R_CHUNK = 64
R_BLOCK = 256
R_GROUP = 256
FF_TILE = 1024
MOD_ROWS = 8
NEG = -1e30
VMEM_LIMIT = 52 * 1024 * 1024


def _cp(*sem):
    return pltpu.CompilerParams(dimension_semantics=sem, vmem_limit_bytes=VMEM_LIMIT)


def _dot(a, b):
    return jnp.dot(a, b, preferred_element_type=F32)


def _dot_nt(a, b):
    return lax.dot_general(a, b, (((1,), (1,)), ((), ())), preferred_element_type=F32)


def _dot_tn(a, b):
    return lax.dot_general(a, b, (((0,), (0,)), ((), ())), preferred_element_type=F32)


def _split2(x):
    hi = x.astype(BF16)
    lo = (x - hi.astype(F32)).astype(BF16)
    return hi, lo


def _split3(x):
    hi = x.astype(BF16)
    r = x - hi.astype(F32)
    mid = r.astype(BF16)
    lo = (r - mid.astype(F32)).astype(BF16)
    return hi, mid, lo


def _sigmoid(x):
    return 1.0 / (1.0 + jnp.exp(-x))


def _log_sigmoid(x):
    return jnp.minimum(x, 0.0) - jnp.log(1.0 + jnp.exp(-jnp.abs(x)))


def _layer_norm(z, g, b):
    mu = jnp.mean(z, axis=-1, keepdims=True)
    zc = z - mu
    var = jnp.mean(zc * zc, axis=-1, keepdims=True)
    return zc * lax.rsqrt(var + LN_EPS) * g + b


def _iota(shape, dim):
    return lax.broadcasted_iota(jnp.int32, shape, dim)


def _group_ones(n, group):
    return jnp.where(_iota((n, n), 0) // group == _iota((n, n), 1) // group, 1.0, 0.0).astype(BF16)


def _ada_body(c_ref, w_ref, b_ref, o_ref):
    x = c_ref[...]
    x = x * _sigmoid(x)
    xh, xl = _split2(x)
    wh, wl = _split2(w_ref[...])
    o_ref[...] = _dot(xh, wh) + _dot(xh, wl) + _dot(xl, wh) + b_ref[...]


def _ada(cvec, ada_w, ada_b):
    depth, d, n = ada_w.shape
    tn = 1024
    return pl.pallas_call(
        _ada_body,
        grid=(depth, n // tn),
        in_specs=[pl.BlockSpec((MOD_ROWS, d), lambda l, j: (0, 0)),
                  pl.BlockSpec((None, d, tn), lambda l, j: (l, 0, j)),
                  pl.BlockSpec((None, 1, tn), lambda l, j: (l, 0, j))],
        out_specs=pl.BlockSpec((None, MOD_ROWS, tn), lambda l, j: (l, 0, j)),
        out_shape=jax.ShapeDtypeStruct((depth, MOD_ROWS, n), F32),
        compiler_params=_cp("parallel", "parallel"),
        name="ada_mod",
    )(cvec, ada_w, ada_b.reshape(depth, 1, n))


def _mod_spec(d, which, row_fn):
    return pl.BlockSpec((None, 1, d), lambda *g: (row_fn(*g), 0, which))


def _proj_body(h_ref, sc_ref, sh_ref, w_ref, ws_ref, o_ref, os_ref, u_ref):
    j = pl.program_id(2)
    last = pl.num_programs(2) - 1

    @pl.when(j == 0)
    def _():
        u_ref[...] = (h_ref[...] * (1.0 + sc_ref[...]) + sh_ref[...]).astype(BF16)

    @pl.when(j < last)
    def _():
        o_ref[...] = _dot(u_ref[...], w_ref[...]).astype(o_ref.dtype)

    @pl.when(j == last)
    def _():
        os_ref[...] = _dot(u_ref[...], ws_ref[...])


def _proj(h, mod, w, w_small, tn, batch, ctx_tile, name):
    b, tt, d = h.shape
    n = w.shape[1]
    ns = w_small.shape[1]
    nj = n // tn
    row = lambda bi, i, j: jnp.where(i == ctx_tile, batch, bi)
    return pl.pallas_call(
        _proj_body,
        grid=(b, tt // TM, nj + 1),
        in_specs=[pl.BlockSpec((None, TM, d), lambda bi, i, j: (bi, i, 0)),
                  _mod_spec(d, 1, row), _mod_spec(d, 0, row),
                  pl.BlockSpec((d, tn), lambda bi, i, j: (0, jnp.minimum(j, nj - 1))),
                  pl.BlockSpec((d, ns), lambda bi, i, j: (0, 0))],
        out_specs=[pl.BlockSpec((None, TM, tn), lambda bi, i, j: (bi, i, jnp.minimum(j, nj - 1))),
                   pl.BlockSpec((None, TM, ns), lambda bi, i, j: (bi, i, 0))],
        out_shape=[jax.ShapeDtypeStruct((b, tt, n), BF16), jax.ShapeDtypeStruct((b, tt, ns), F32)],
        scratch_shapes=[pltpu.VMEM((TM, d), BF16)],
        compiler_params=_cp("parallel", "parallel", "arbitrary"),
        name=name,
    )(h, mod, mod, w, w_small)


def _seq_block(c, direction, n_lat, n_ctx):
    if direction == 0:
        return jnp.where(c < n_ctx, n_lat + c, c - n_ctx)
    return jnp.where(c < n_ctx, n_lat + n_ctx - 1 - c, n_lat - 1 - (c - n_ctx))


def _mlstm_body(direction, n_lat, n_ctx,
                qk_ref, qkp_ref, qkn_ref, v_ref, g_ref, gt_ref, cw_ref, gb_ref, gbt_ref,
                o_ref, ct_ref, n_ref, m_ref):
    L = M_CHUNK
    c = pl.program_id(1)
    n_live = n_lat + n_ctx

    @pl.when(c == 0)
    def _():
        ct_ref[...] = jnp.zeros_like(ct_ref)
        n_ref[...] = jnp.zeros_like(n_ref)
        m_ref[...] = jnp.zeros_like(m_ref)

    @pl.when(c >= n_live)
    def _():
        o_ref[...] = jnp.zeros_like(o_ref)

    @pl.when(c < n_live)
    def _():
        blk = _seq_block(c, direction, n_lat, n_ctx)
        first = jnp.logical_or(blk == 0, blk == n_lat)
        last = jnp.logical_or(blk == n_lat - 1, blk == n_live - 1)
        p = qk_ref[...].astype(F32)
        rows = _iota((L, 1), 0)
        prev_row = jnp.where(first, 0.0, qkp_ref[HALO - 1:HALO, :].astype(F32))
        next_row = jnp.where(last, 0.0, qkn_ref[0:1, :].astype(F32))
        pm = jnp.where(rows == 0, prev_row, pltpu.roll(p, 1, 0))
        pp = jnp.where(rows == L - 1, next_row, pltpu.roll(p, L - 1, 0))
        qkc = cw_ref[0:1, :] * pm + cw_ref[1:2, :] * p + cw_ref[2:3, :] * pp
        qk = qkc * _sigmoid(qkc)
        nq = M_HEADS * M_DQK
        q_all = qk[:, :nq]
        k_all = qk[:, nq:] * (M_DQK ** -0.5)
        v_all = v_ref[...]

        g = g_ref[...] + gb_ref[...]
        gt = gt_ref[...] + gbt_ref[...]
        r_i = _iota((L, L), 0)
        c_i = _iota((L, L), 1)
        causal = (r_i >= c_i) if direction == 0 else (r_i <= c_i)
        tri = jnp.where(causal, 1.0, 0.0).astype(BF16)
        tri_t = jnp.where(causal, 0.0, 1.0).astype(BF16) + jnp.where(r_i == c_i, 1.0, 0.0).astype(BF16)
        lf = _log_sigmoid(g)
        lft = _log_sigmoid(gt)
        b_col = sum(_dot(tri, part) for part in _split3(lf))
        b_row = sum(_dot(part, tri_t) for part in _split3(lft))
        e_last = L - 1 if direction == 0 else 0

        hs = []
        for hd in range(M_HEADS):
            ci = 16 * direction + hd
            cf = ci + 8
            q_h = q_all[:, hd * M_DQK:(hd + 1) * M_DQK]
            q_b = q_h.astype(BF16)
            k_b = k_all[:, hd * M_DQK:(hd + 1) * M_DQK].astype(BF16)
            hs.append(dict(b_c=b_col[:, cf:cf + 1], i_c=g[:, ci:ci + 1], b_r=b_row[cf:cf + 1, :],
                           i_r=gt[ci:ci + 1, :], m_st=m_ref[hd][0:1, 0:1], n_st=n_ref[hd][0:1, :],
                           q_h=q_h, q_b=q_b, k_b=k_b, v_h=v_all[:, hd * M_DV:(hd + 1) * M_DV].astype(F32),
                           qk=_dot_nt(q_b, k_b), qc=_dot(q_b, ct_ref[hd].astype(BF16))))
        for s in hs:
            dmat = jnp.where(causal, s["b_c"] - s["b_r"] + s["i_r"], NEG)
            inter = s["b_c"] + s["m_st"]
            s["m_t"] = jnp.maximum(inter, jnp.max(dmat, axis=-1, keepdims=True))
            s["s"] = s.pop("qk") * jnp.exp(dmat - s["m_t"])
            s["w_inter"] = jnp.exp(inter - s["m_t"])
        for s in hs:
            s["sv"] = _dot(s["s"].astype(BF16), s["v_h"].astype(BF16))
        for hd, s in enumerate(hs):
            num = s.pop("sv") + s["w_inter"] * s.pop("qc")
            den = (jnp.sum(s.pop("s"), axis=-1, keepdims=True)
                   + s.pop("w_inter") * jnp.sum(s.pop("q_h") * s["n_st"], axis=-1, keepdims=True))
            o_ref[:, hd * M_DV:(hd + 1) * M_DV] = num / jnp.maximum(jnp.abs(den), jnp.exp(-s.pop("m_t")))
        for s in hs:
            b_last = s["b_c"][e_last:e_last + 1, :]
            g_c = b_last - s["b_c"] + s["i_c"]
            g_r = b_last - s["b_r"] + s["i_r"]
            s["m_new"] = jnp.maximum(b_last + s["m_st"], jnp.max(g_r, axis=-1, keepdims=True))
            s["decay"] = jnp.exp(b_last + s["m_st"] - s["m_new"])
            s["kv"] = _dot_tn(s["k_b"], (jnp.exp(g_c - s["m_new"]) * s["v_h"]).astype(BF16))
            s["kn"] = _dot(jnp.broadcast_to(jnp.exp(g_r - s["m_new"]), (8, L)).astype(BF16), s["k_b"])[0:1, :]
        for hd, s in enumerate(hs):
            ct_ref[hd] = s["decay"] * ct_ref[hd] + s["kv"]
            n_ref[hd] = jnp.broadcast_to(s["decay"] * s["n_st"] + s["kn"], (8, M_DQK))
            m_ref[hd] = jnp.broadcast_to(s["m_new"], (8, 128))


def _mlstm_scan(p, gates, gates_t, conv_w, gb_row, gb_col, direction, s_len, c_len):
    b, tt, _ = p.shape
    L = M_CHUNK
    n_lat, n_ctx = s_len // L, c_len // L
    n_all = tt // L
    nqk = 2 * M_HEADS * M_DQK
    nv = M_HEADS * M_DV

    def blk(c):
        return jnp.where(c < n_lat + n_ctx, _seq_block(c, direction, n_lat, n_ctx), c)

    rh = L // HALO
    return pl.pallas_call(
        functools.partial(_mlstm_body, direction, n_lat, n_ctx),
        grid=(b, n_all),
        in_specs=[pl.BlockSpec((None, L, nqk), lambda bi, c: (bi, blk(c), 0)),
                  pl.BlockSpec((None, HALO, nqk), lambda bi, c: (bi, jnp.maximum(blk(c) * rh - 1, 0), 0)),
                  pl.BlockSpec((None, HALO, nqk),
                               lambda bi, c: (bi, jnp.minimum((blk(c) + 1) * rh, tt // HALO - 1), 0)),
                  pl.BlockSpec((None, L, nv), lambda bi, c: (bi, blk(c), 1)),
                  pl.BlockSpec((None, L, 128), lambda bi, c: (bi, blk(c), 0)),
                  pl.BlockSpec((None, 32, L), lambda bi, c: (bi, 0, blk(c))),
                  pl.BlockSpec((3, nqk), lambda bi, c: (0, 0)),
                  pl.BlockSpec((1, 128), lambda bi, c: (0, 0)),
                  pl.BlockSpec((32, 1), lambda bi, c: (0, 0))],
        out_specs=pl.BlockSpec((None, L, nv), lambda bi, c: (bi, blk(c), 0)),
        out_shape=jax.ShapeDtypeStruct((b, tt, nv), F32),
        scratch_shapes=[pltpu.VMEM((M_HEADS, M_DQK, M_DV), F32),
                        pltpu.VMEM((M_HEADS, 8, M_DQK), F32),
                        pltpu.VMEM((M_HEADS, 8, 128), F32)],
        compiler_params=_cp("parallel", "arbitrary"),
        name="mlstm_scan_%d" % direction,
    )(p, p, p, p, gates, gates_t, conv_w, gb_row, gb_col)


def _out_epilogue(z_b, w_ref, res_ref, gate_ref, lg_ref, lb_ref, o_ref):
    y = _dot(z_b, w_ref[...])
    o_ref[...] = _layer_norm(ALPHA * res_ref[...] + gate_ref[...] * y, lg_ref[...], lb_ref[...])


def _mlstm_out_body(hf_ref, hb_ref, og_ref, nw_ref, w_ref, res_ref, gate_ref, lg_ref, lb_ref, o_ref, z_ref):
    ones = jnp.full((M_DV, M_DV), 1.0, BF16)
    for hd in range(M_HEADS):
        sl = slice(hd * M_DV, (hd + 1) * M_DV)
        hh = hf_ref[:, sl] + hb_ref[:, sl]
        mu = _dot(hh.astype(BF16), ones) * (1.0 / M_DV)
        hc = hh - mu
        var = _dot((hc * hc).astype(BF16), ones) * (1.0 / M_DV)
        hn = hc * lax.rsqrt(var + M_NORM_EPS) * nw_ref[:, sl]
        z_ref[:, sl] = (_sigmoid(og_ref[:, sl].astype(F32)) * hn).astype(BF16)
    _out_epilogue(z_ref[...], w_ref, res_ref, gate_ref, lg_ref, lb_ref, o_ref)


def _rwkv_out_body(yf_ref, yb_ref, y0_ref, bon_ref, gt_ref, lw_ref, lbx_ref,
                   w_ref, res_ref, gate_ref, lg_ref, lb_ref, o_ref, z_ref):
    ones = _group_ones(R_GROUP, R_HEAD)
    d = o_ref.shape[-1]
    for gi in range(d // R_GROUP):
        sl = slice(gi * R_GROUP, (gi + 1) * R_GROUP)
        y = yf_ref[:, sl].astype(F32) + yb_ref[:, sl].astype(F32) + y0_ref[:, sl]
        mu = _dot(y.astype(BF16), ones) * (1.0 / R_HEAD)
        yc = y - mu
        var = _dot((yc * yc).astype(BF16), ones) * (1.0 / R_HEAD)
        yn = yc * lax.rsqrt(var + R_GN_EPS) * lw_ref[:, sl] + lbx_ref[:, sl]
        z_ref[:, sl] = ((yn + bon_ref[:, sl]) * gt_ref[:, sl]).astype(BF16)
    _out_epilogue(z_ref[...], w_ref, res_ref, gate_ref, lg_ref, lb_ref, o_ref)


def _out_call(body, streams, stream_col, vecs, w, res, mod, ln_g, ln_b, batch, s_len, n_tiles, name):
    b, _, d = res.shape
    row = lambda bi, i: jnp.where(i * TMO >= s_len, batch, bi)
    tile = lambda col: pl.BlockSpec((None, TMO, d), lambda bi, i: (bi, i, col))
    vec = pl.BlockSpec((1, d), lambda bi, i: (0, 0))
    return pl.pallas_call(
        body,
        grid=(b, n_tiles),
        in_specs=[tile(col) for col in stream_col] + [vec] * len(vecs)
                 + [pl.BlockSpec((d, d), lambda bi, i: (0, 0)), tile(0), _mod_spec(d, 2, row), vec, vec],
        out_specs=tile(0),
        out_shape=jax.ShapeDtypeStruct((b, n_tiles * TMO, d), F32),
        scratch_shapes=[pltpu.VMEM((TMO, d), BF16)],
        compiler_params=_cp("parallel", "parallel"),
        name=name,
    )(*streams, *vecs, w, res, mod, ln_g, ln_b)


def _mlp_body(h_ref, sc_ref, sh_ref, gate_ref, w1_ref, w2_ref, lg_ref, lb_ref, o_ref, u_ref, acc_ref):
    k = pl.program_id(2)

    @pl.when(k == 0)
    def _():
        u_ref[...] = (h_ref[...] * (1.0 + sc_ref[...]) + sh_ref[...]).astype(BF16)
        acc_ref[...] = jnp.zeros_like(acc_ref)

    a = jnp.maximum(_dot(u_ref[...], w1_ref[...]), 0.0)
    acc_ref[...] += _dot((a * a).astype(BF16), w2_ref[...])

    @pl.when(k == pl.num_programs(2) - 1)
    def _():
        o_ref[...] = _layer_norm(ALPHA * h_ref[...] + gate_ref[...] * acc_ref[...], lg_ref[...], lb_ref[...])


def _mlp(h, mod, w1, w2, ln_g, ln_b, batch, ctx_tile, n_tiles, name):
    b, _, d = h.shape
    dff = w1.shape[1]
    row = lambda bi, i, k: jnp.where(i == ctx_tile, batch, bi)
    tile = pl.BlockSpec((None, TM, d), lambda bi, i, k: (bi, i, 0))
    vec = pl.BlockSpec((1, d), lambda bi, i, k: (0, 0))
    return pl.pallas_call(
        _mlp_body,
        grid=(b, n_tiles, dff // FF_TILE),
        in_specs=[tile, _mod_spec(d, 4, row), _mod_spec(d, 3, row), _mod_spec(d, 5, row),
                  pl.BlockSpec((d, FF_TILE), lambda bi, i, k: (0, k)),
                  pl.BlockSpec((FF_TILE, d), lambda bi, i, k: (k, 0)), vec, vec],
        out_specs=tile,
        out_shape=jax.ShapeDtypeStruct((b, n_tiles * TM, d), F32),
        scratch_shapes=[pltpu.VMEM((TM, d), BF16), pltpu.VMEM((TM, d), F32)],
        compiler_params=_cp("parallel", "parallel", "arbitrary"),
        name=name,
    )(h, mod, mod, mod, w1, w2, ln_g, ln_b)


def _rproj_body(ctx_tile, c_len, n_lat_tiles,
                h_ref, hp_ref, hn_ref, sc_ref, sh_ref, mu_ref, w_ref, wl_ref, o_ref, ol_ref, u_ref, xx_ref):
    i = pl.program_id(1)

    @pl.when(pl.program_id(2) == 0)
    def _():
        is_ctx = i == ctx_tile
        sc = 1.0 + sc_ref[...]
        sh = sh_ref[...]
        u = h_ref[...] * sc + sh
        u_prev = hp_ref[...] * sc + sh
        u_next = hn_ref[...] * sc + sh
        u_ref[...] = u
        t = _iota((TM, 1), 0)
        tw = jnp.bitwise_and(t, GRID_W - 1)
        q = u.shape[1] // 4
        flag = lambda cond: jnp.where(cond, 1, 0)
        ctx_prev, ctx_next = flag(t != 0), flag(t != c_len - 1)
        lat_left, lat_right = flag(tw != 0), flag(tw != GRID_W - 1)
        lat_up = jnp.where(i == 0, flag(t >= GRID_W), 1)
        lat_down = jnp.where(i == n_lat_tiles - 1, flag(t < TM - GRID_W), 1)
        for qi in range(4):
            sl = slice(qi * q, (qi + 1) * q)
            uq = u[:, sl]
            before = pltpu.roll(uq, 1, 0)
            after = pltpu.roll(uq, TM - 1, 0)
            if qi == 0:
                src = before
                keep = jnp.where(is_ctx, ctx_prev, lat_left)
            elif qi == 1:
                src = jnp.where(is_ctx, before, after)
                keep = jnp.where(is_ctx, ctx_prev, lat_right)
            elif qi == 2:
                above = jnp.concatenate([u_prev[:, sl], uq[:TM - GRID_W]], axis=0)
                src = jnp.where(is_ctx, after, above)
                keep = jnp.where(is_ctx, ctx_next, lat_up)
            else:
                below = jnp.concatenate([uq[GRID_W:], u_next[:, sl]], axis=0)
                src = jnp.where(is_ctx, after, below)
                keep = jnp.where(is_ctx, ctx_next, lat_down)
            xx_ref[:, sl] = jnp.where(keep != 0, src, 0.0) - uq

    j = pl.program_id(2)
    last = pl.num_programs(2) - 1

    @pl.when(j < last)
    def _():
        x = (u_ref[...] + xx_ref[...] * mu_ref[j]).astype(BF16)
        o_ref[...] = _dot(x, w_ref[...]).astype(o_ref.dtype)

    @pl.when(j == last)
    def _():
        for ti in range(ol_ref.shape[-1] // R_LORA_PAD):
            sl = slice(ti * R_LORA_PAD, (ti + 1) * R_LORA_PAD)
            x = (u_ref[...] + xx_ref[...] * mu_ref[3 + max(ti - 1, 0)]).astype(BF16)
            ol_ref[:, sl] = _dot(x, wl_ref[:, sl])


def _rproj(h, mod, mu, w, w_lora, batch, s_len, c_len):
    b, tt, d = h.shape
    n = w.shape[1]
    nl = w_lora.shape[1]
    nj = n // d
    ctx_tile = s_len // TM
    row = lambda bi, i, j: jnp.where(i == ctx_tile, batch, bi)
    r64 = TM // GRID_W
    return pl.pallas_call(
        functools.partial(_rproj_body, ctx_tile, c_len, s_len // TM),
        grid=(b, tt // TM, nj + 1),
        in_specs=[pl.BlockSpec((None, TM, d), lambda bi, i, j: (bi, i, 0)),
                  pl.BlockSpec((None, GRID_W, d), lambda bi, i, j: (bi, jnp.maximum(i * r64 - 1, 0), 0)),
                  pl.BlockSpec((None, GRID_W, d),
                               lambda bi, i, j: (bi, jnp.minimum((i + 1) * r64, tt // GRID_W - 1), 0)),
                  _mod_spec(d, 1, row), _mod_spec(d, 0, row),
                  pl.BlockSpec((8, 1, d), lambda bi, i, j: (0, 0, 0)),
                  pl.BlockSpec((d, d), lambda bi, i, j: (0, jnp.minimum(j, nj - 1))),
                  pl.BlockSpec((d, nl), lambda bi, i, j: (0, 0))],
        out_specs=[pl.BlockSpec((None, TM, d), lambda bi, i, j: (bi, i, jnp.minimum(j, nj - 1))),
                   pl.BlockSpec((None, TM, nl), lambda bi, i, j: (bi, i, 0))],
        out_shape=[jax.ShapeDtypeStruct((b, tt, n), BF16), jax.ShapeDtypeStruct((b, tt, nl), F32)],
        scratch_shapes=[pltpu.VMEM((TM, d), F32), pltpu.VMEM((TM, d), F32)],
        compiler_params=_cp("parallel", "parallel", "arbitrary"),
        name="rwkv_proj",
    )(h, h, h, mod, mod, mu, w, w_lora)


def _block_diag(x, mask):
    return jnp.where(mask, jnp.concatenate([x] * (R_GROUP // R_CHUNK), axis=0), 0.0)


def _rloc_body(r_ref, k_ref, v_ref, lo_ref, g2_ref, w2_ref, a2_ref, vec_ref,
               rf_ref, rb_ref, y0_ref, bon_ref, gate_ref, gf_ref, gb_ref, hf_ref, hb_ref, dgf_ref, dgb_ref,
               rs_ref, vs_ref, kk_ref, lw_ref, kd_ref, bb_ref):
    L = R_CHUNK
    W = R_GROUP
    lo = lo_ref[...]
    r = r_ref[...].astype(F32)
    k = k_ref[...].astype(F32)
    v = v_ref[...].astype(F32)
    w0 = (vec_ref[0:1, :], vec_ref[1:2, :])
    a0 = (vec_ref[2:3, :], vec_ref[3:4, :])
    k_k = vec_ref[4:5, :]
    k_a = vec_ref[5:6, :]
    r_k = vec_ref[6:7, :]
    ones = _group_ones(W, R_HEAD)

    gate_ref[...] = _dot(_sigmoid(lo[:, :R_GATE_LORA]).astype(BF16), g2_ref[...])
    kraw = k * k_k
    ss = _dot((kraw * kraw).astype(BF16), ones)
    kk = kraw / jnp.maximum(jnp.sqrt(ss), 1e-12)
    rs_ref[...] = r
    vs_ref[...] = v
    kk_ref[...] = kk
    kd_sum = jnp.zeros_like(k)
    for d in range(2):
        wl = lo[:, R_GATE_LORA + d * R_LORA_PAD:R_GATE_LORA + (d + 1) * R_LORA_PAD]
        al = lo[:, R_GATE_LORA + (2 + d) * R_LORA_PAD:R_GATE_LORA + (3 + d) * R_LORA_PAD]
        lwx = w0[d] + _dot(jnp.tanh(wl).astype(BF16), w2_ref[d])
        lw_ref[d] = -_sigmoid(lwx) * (jnp.e ** -0.5)
        iclr = _sigmoid(a0[d] + _dot(al.astype(BF16), a2_ref[d]))
        kd = k * (1.0 + (iclr - 1.0) * k_a)
        kd_ref[d] = kd
        bb_ref[d] = kk * iclr
        kd_sum = kd_sum + kd
    bon_ref[...] = _dot((r * r_k * kd_sum).astype(BF16), ones) * v

    lane_head = _iota((W, W), 1) // R_HEAD
    bd_mask = _iota((W, W), 0) // L == lane_head
    t_i = _iota((L, W), 0)
    s_i = jnp.bitwise_and(_iota((L, W), 1), L - 1)
    eye = jnp.where(t_i == s_i, 1.0, 0.0)
    ts_xor = jnp.bitwise_xor(t_i, s_i)
    diag2 = jnp.right_shift(ts_xor, 1) == 0
    off_masks = [jnp.right_shift(ts_xor, lv) == 1 for lv in range(1, 6)]
    head_of_lane = _iota((L, W), 1) // R_HEAD
    r_i = _iota((L, L), 0)
    c_i = _iota((L, L), 1)
    outs = ((rf_ref, gf_ref, hf_ref), (rb_ref, gb_ref, hb_ref))

    tris = (jnp.where(r_i >= c_i, 1.0, 0.0).astype(BF16), jnp.where(r_i <= c_i, 1.0, 0.0).astype(BF16))
    stricts = (t_i > s_i, t_i < s_i)
    incls = (t_i >= s_i, t_i <= s_i)
    bd = lambda x: _block_diag(x, bd_mask).astype(BF16)

    n_chunks = R_BLOCK // L
    inst = [(ci, d) for ci in range(n_chunks) for d in range(2)]
    rows_of = [pl.ds(ci * L, L) for ci in range(n_chunks)]
    vcs = [vs_ref[rows_of[ci], :] for ci in range(n_chunks)]
    bd_vs = [bd(vc) for vc in vcs]

    st = []
    for ci, d in inst:
        rows = rows_of[ci]
        lwc = lw_ref[d, rows, :]
        cum = sum(_dot(tris[d], part) for part in _split3(lwc))
        e_last = L - 1 if d == 0 else 0
        tot = cum[e_last:e_last + 1, :]
        p_inv = jnp.exp(-cum)
        p_end = jnp.exp(tot - cum)
        bc = bb_ref[d, rows, :]
        kdc = kd_ref[d, rows, :]
        st.append(dict(tot=tot, a_t=-kk_ref[rows, :] * jnp.exp(cum - lwc), r_t=rs_ref[rows, :] * jnp.exp(cum),
                       b_inv=bc * p_inv, k_inv=kdc * p_inv,
                       b_end=(bc * p_end).astype(BF16), k_end=(kdc * p_end).astype(BF16)))
    for s, (ci, d) in zip(st, inst):
        z = _dot_nt(jnp.concatenate([s["a_t"], s["r_t"]], axis=0).astype(BF16),
                    jnp.concatenate([bd(s.pop("b_inv")), bd(s.pop("k_inv"))], axis=0))
        s["a_ab"] = jnp.where(stricts[d], z[:L, :W], 0.0)
        s["a_ak"] = jnp.where(stricts[d], z[:L, W:], 0.0).astype(BF16)
        s["m_rb"] = jnp.where(incls[d], z[L:, :W], 0.0).astype(BF16)
        s["m_rk"] = jnp.where(incls[d], z[L:, W:], 0.0).astype(BF16)
        s["tinv"] = eye + jnp.where(diag2, s["a_ab"], 0.0)
    for s, (ci, d) in zip(st, inst):
        s["akv"] = _dot(s.pop("a_ak"), bd_vs[ci])
    for off in off_masks:
        for s in st:
            s["x1"] = _dot(s["tinv"].astype(BF16), bd(jnp.where(off, s["a_ab"], 0.0)))
        for s in st:
            s["tinv"] = s["tinv"] + _dot(s.pop("x1").astype(BF16), bd(s["tinv"]))
    for s in st:
        x = _dot(s.pop("tinv").astype(BF16), jnp.concatenate([bd(s.pop("a_t")), bd(s.pop("akv"))], axis=1))
        s["a_p"] = x[:, :W]
        s["u0"] = x[:, W:]
    y0s = [jnp.zeros((L, W), F32) for _ in range(n_chunks)]
    for s, (ci, d) in zip(st, inst):
        y2 = _dot(s.pop("m_rb"), jnp.concatenate([bd(s["a_p"]), bd(s["u0"])], axis=1))
        outs[d][0][rows_of[ci], :] = (s.pop("r_t") + y2[:, :W]).astype(BF16)
        y0s[ci] = y0s[ci] + y2[:, W:] + _dot(s.pop("m_rk"), bd_vs[ci])
    for ci in range(n_chunks):
        y0_ref[rows_of[ci], :] = y0s[ci]
    for s, (ci, d) in zip(st, inst):
        zs = _dot_tn(s.pop("b_end"), jnp.concatenate([s.pop("a_p"), s.pop("u0")], axis=1).astype(BF16))
        zk = _dot_tn(s.pop("k_end"), vcs[ci].astype(BF16))
        g_full = zs[:, :W]
        h_full = zs[:, W:] + zk
        g_sbs = jnp.zeros((L, W), F32)
        h_sbs = jnp.zeros((L, W), F32)
        for hd in range(W // R_HEAD):
            sel = head_of_lane == hd
            g_sbs = g_sbs + jnp.where(sel, g_full[hd * R_HEAD:(hd + 1) * R_HEAD, :], 0.0)
            h_sbs = h_sbs + jnp.where(sel, h_full[hd * R_HEAD:(hd + 1) * R_HEAD, :], 0.0)
        outs[d][1][rows_of[ci], :] = g_sbs.astype(BF16)
        outs[d][2][rows_of[ci], :] = h_sbs.astype(BF16)
    for dg_ref in (dgf_ref, dgb_ref):
        dg_ref[n_chunks:, :] = jnp.zeros((8 - n_chunks, W), F32)
    for s, (ci, d) in zip(st, inst):
        (dgf_ref, dgb_ref)[d][ci:ci + 1, :] = jnp.exp(s["tot"])


def _rloc(rkv, lora, g2, w2p, a2p, vecs, s_len, c_len):
    b, tt, n3 = rkv.shape
    d = n3 // 3
    nb = (s_len + c_len) // R_BLOCK
    ng = d // R_GROUP
    tok = lambda col: pl.BlockSpec((None, R_BLOCK, R_GROUP), lambda bi, i, j: (bi, i, col * ng + j))
    out_tok = pl.BlockSpec((None, R_BLOCK, R_GROUP), lambda bi, i, j: (bi, i, j))
    f32_out = jax.ShapeDtypeStruct((b, tt, d), F32)
    bf_out = jax.ShapeDtypeStruct((b, tt, d), BF16)
    sc = lambda *shape: pltpu.VMEM(shape, F32)
    return pl.pallas_call(
        _rloc_body,
        grid=(b, nb, ng),
        in_specs=[tok(0), tok(1), tok(2),
                  pl.BlockSpec((None, R_BLOCK, lora.shape[2]), lambda bi, i, j: (bi, i, 0)),
                  pl.BlockSpec((R_GATE_LORA, R_GROUP), lambda bi, i, j: (0, j)),
                  pl.BlockSpec((2, R_LORA_PAD, R_GROUP), lambda bi, i, j: (0, 0, j)),
                  pl.BlockSpec((2, R_LORA_PAD, R_GROUP), lambda bi, i, j: (0, 0, j)),
                  pl.BlockSpec((8, R_GROUP), lambda bi, i, j: (0, j))],
        out_specs=[out_tok] * 9 + [pl.BlockSpec((None, None, 8, R_GROUP), lambda bi, i, j: (bi, i, 0, j))] * 2,
        out_shape=[bf_out, bf_out, f32_out, f32_out, f32_out, bf_out, bf_out, bf_out, bf_out,
                   jax.ShapeDtypeStruct((b, nb, 8, d), F32), jax.ShapeDtypeStruct((b, nb, 8, d), F32)],
        scratch_shapes=[sc(R_BLOCK, R_GROUP), sc(R_BLOCK, R_GROUP), sc(R_BLOCK, R_GROUP),
                        sc(2, R_BLOCK, R_GROUP), sc(2, R_BLOCK, R_GROUP), sc(2, R_BLOCK, R_GROUP)],
        compiler_params=_cp("parallel", "parallel", "arbitrary"),
        name="rwkv_local",
    )(rkv, rkv, rkv, lora, g2, w2p, a2p, vecs)


def _rseq_body(n_lat, n_ctx, gf_ref, gb_ref, hf_ref, hb_ref, rf_ref, rb_ref, dgf_ref, dgb_ref,
               yf_ref, yb_ref, st_ref):
    L = R_CHUNK
    W = R_GROUP
    c = pl.program_id(1)

    @pl.when(c == 0)
    def _():
        st_ref[...] = jnp.zeros_like(st_ref)

    bd_mask = _iota((W, W), 0) // L == _iota((W, W), 1) // R_HEAD
    eye = jnp.where(_iota((L, W), 0) == jnp.bitwise_and(_iota((L, W), 1), L - 1), 1.0, 0.0)
    ins = ((gf_ref, hf_ref, rf_ref, dgf_ref, yf_ref), (gb_ref, hb_ref, rb_ref, dgb_ref, yb_ref))
    for d in range(2):
        g_ref, ha_ref, rp_ref, dg_ref, y_ref = ins[d]
        chunk_in_block = jnp.bitwise_and(_seq_block(c, d, n_lat, n_ctx), R_BLOCK // L - 1)
        for gi in range(st_ref.shape[-1] // W):
            sl = slice(gi * W, (gi + 1) * W)
            st = st_ref[d, :, sl]
            st_hi = st.astype(BF16)
            st_lo = (st - st_hi.astype(F32)).astype(BF16)
            g_hi, g_lo = _split2(g_ref[:, sl].astype(F32) + eye * dg_ref[pl.ds(chunk_in_block, 1), sl])
            rp = rp_ref[:, sl]
            bd_hi = _block_diag(st_hi.astype(F32), bd_mask).astype(BF16)
            bd_lo = _block_diag(st_lo.astype(F32), bd_mask).astype(BF16)
            o1 = _dot(jnp.concatenate([g_hi, g_lo, rp], axis=0), bd_hi)
            o2 = _dot(jnp.concatenate([g_hi, rp], axis=0), bd_lo)
            st_ref[d, :, sl] = o1[:L] + o1[L:2 * L] + o2[:L] + ha_ref[:, sl].astype(F32)
            y_ref[:, sl] = (o1[2 * L:] + o2[L:]).astype(y_ref.dtype)


def _rseq(gf, gb, hf, hb, rf, rb, dgf, dgb, s_len, c_len):
    b, tt, d = gf.shape
    L = R_CHUNK
    n_lat, n_ctx = s_len // L, c_len // L
    per_block = R_BLOCK // L
    spec = lambda direction: pl.BlockSpec(
        (None, L, d), lambda bi, c: (bi, _seq_block(c, direction, n_lat, n_ctx), 0))
    dspec = lambda direction: pl.BlockSpec(
        (None, None, 8, d), lambda bi, c: (bi, _seq_block(c, direction, n_lat, n_ctx) // per_block, 0, 0))
    out = jax.ShapeDtypeStruct((b, tt, d), BF16)
    return pl.pallas_call(
        functools.partial(_rseq_body, n_lat, n_ctx),
        grid=(b, n_lat + n_ctx),
        in_specs=[spec(0), spec(1), spec(0), spec(1), spec(0), spec(1), dspec(0), dspec(1)],
        out_specs=[spec(0), spec(1)],
        out_shape=[out, out],
        scratch_shapes=[pltpu.VMEM((2, L, d), F32)],
        compiler_params=_cp("parallel", "arbitrary"),
        name="rwkv_seq",
    )(gf, gb, hf, hb, rf, rb, dgf, dgb)


def kernel(x, c, ctx, c_ctx, ada_w, ada_b, ln_g, ln_b, mlp_w1, mlp_w2, m_w_in, m_conv, m_gate_b, m_norm_w, m_w_out, r_w_in, r_mu, r_g2, r_w0, r_w2, r_a0, r_a2, r_k_k, r_k_a, r_r_k, r_lnx_w, r_lnx_b, r_w_out):
    batch, s_len, d = x.shape
    c_len = ctx.shape[1]
    assert batch < MOD_ROWS and s_len % TM == 0 and c_len % M_CHUNK == 0 and c_len <= TM
    assert s_len % GRID_W == 0 and d % R_GROUP == 0 and ada_w.shape[0] == DEPTH == 2
    tt = s_len + TM
    ctx_tile = s_len // TM
    row = lambda a: a.reshape(1, -1)

    h = jnp.concatenate([x, ctx, jnp.zeros((batch, TM - c_len, d), F32)], axis=1)
    cvec = jnp.concatenate([c, c_ctx[None], jnp.zeros((MOD_ROWS - batch - 1, d), F32)], axis=0)
    mod = _ada(cvec, ada_w, ada_b).reshape(DEPTH, MOD_ROWS, 1, 6 * d)

    nqk = 2 * M_HEADS * M_DQK
    nv = M_HEADS * M_DV
    w_in = m_w_in[0]
    w_gates = jnp.pad(w_in[:, nqk + 2 * nv:], ((0, 0), (0, 128 - 4 * M_HEADS))).astype(BF16)
    p, gates = _proj(h, mod[0], w_in[:, :nqk + 2 * nv].astype(BF16), w_gates, 1024, batch, ctx_tile, "mlstm_proj")
    gates_t = jnp.swapaxes(gates[:, :, :4 * M_HEADS], 1, 2)
    gb = m_gate_b[0].reshape(-1)
    gb_row = jnp.pad(gb, (0, 128 - gb.shape[0])).reshape(1, 128)
    gb_col = gb.reshape(-1, 1)
    scan = functools.partial(_mlstm_scan, p, gates, gates_t, m_conv[0], gb_row, gb_col)
    h_f = scan(0, s_len, c_len)
    h_b = scan(1, s_len, c_len)
    h = _out_call(_mlstm_out_body, (h_f, h_b, p), (0, 0, (nqk + nv) // d), (row(m_norm_w[0]),),
                  m_w_out[0].astype(BF16), h, mod[0], row(ln_g[0, 0]), row(ln_b[0, 0]),
                  batch, s_len, tt // TMO, "mlstm_out")
    h = _mlp(h, mod[0], mlp_w1[0].astype(BF16), mlp_w2[0].astype(BF16), row(ln_g[0, 1]), row(ln_b[0, 1]),
             batch, ctx_tile, tt // TM, "mlp0")

    w_in = r_w_in[0]
    mu = r_mu[0].reshape(8, 1, d)
    lo_cols = [w_in[:, 3 * d:3 * d + R_GATE_LORA]]
    for gi in range(4):
        start = 3 * d + R_GATE_LORA + gi * R_LORA
        lo_cols.append(jnp.pad(w_in[:, start:start + R_LORA], ((0, 0), (0, R_LORA_PAD - R_LORA))))
    w_lora = jnp.concatenate(lo_cols, axis=1).astype(BF16)
    rkv, lora = _rproj(h, mod[1], mu, w_in[:, :3 * d].astype(BF16), w_lora, batch, s_len, c_len)
    pad_rows = ((0, 0), (0, R_LORA_PAD - R_LORA), (0, 0))
    vecs = jnp.concatenate([r_w0[0], r_a0[0], row(r_k_k[0]), row(r_k_a[0]), row(r_r_k[0]),
                            jnp.zeros((1, d), F32)], axis=0)
    rf, rb, y0, bonus, gate, g_f, g_b, ha_f, ha_b, dg_f, dg_b = _rloc(
        rkv, lora, r_g2[0].astype(BF16), jnp.pad(r_w2[0], pad_rows).astype(BF16),
        jnp.pad(r_a2[0], pad_rows).astype(BF16), vecs, s_len, c_len)
    y_f, y_b = _rseq(g_f, g_b, ha_f, ha_b, rf, rb, dg_f, dg_b, s_len, c_len)
    h1 = _out_call(_rwkv_out_body, (y_f, y_b, y0, bonus, gate), (0, 0, 0, 0, 0),
                   (row(r_lnx_w[0]), row(r_lnx_b[0])), r_w_out[0].astype(BF16), h, mod[1],
                   row(ln_g[1, 0]), row(ln_b[1, 0]), batch, s_len, s_len // TMO, "rwkv_out")
    return _mlp(h1, mod[1], mlp_w1[1].astype(BF16), mlp_w2[1].astype(BF16), row(ln_g[1, 1]), row(ln_b[1, 1]),
                batch, ctx_tile, s_len // TM, "mlp1")
```

```python
import functools

import jax
import jax.numpy as jnp
from jax import lax
from jax.experimental import pallas as pl
from jax.experimental.pallas import tpu as pltpu

F32 = jnp.float32
BF16 = jnp.bfloat16

DEPTH = 2
GRID_W = 64
ALPHA = (2 * DEPTH) ** 0.25
LN_EPS = 1e-5
M_HEADS = 8
M_DQK = 128
M_DV = 256
M_NORM_EPS = 1e-6
R_HEAD = 64
R_GN_EPS = 64e-5
R_LORA = 96
R_LORA_PAD = 128
R_GATE_LORA = 256

TM = 512
TMO = 256
M_CHUNK = 256
HALO = 16
R_CHUNK = 64
R_BLOCK = 256
R_GROUP = 256
FF_TILE = 1024
MOD_ROWS = 8
NEG = -1e30
VMEM_LIMIT = 52 * 1024 * 1024


def _cp(*sem):
    return pltpu.CompilerParams(dimension_semantics=sem, vmem_limit_bytes=VMEM_LIMIT)


def _dot(a, b):
    return jnp.dot(a, b, preferred_element_type=F32)


def _dot_nt(a, b):
    return lax.dot_general(a, b, (((1,), (1,)), ((), ())), preferred_element_type=F32)


def _dot_tn(a, b):
    return lax.dot_general(a, b, (((0,), (0,)), ((), ())), preferred_element_type=F32)


def _split2(x):
    hi = x.astype(BF16)
    lo = (x - hi.astype(F32)).astype(BF16)
    return hi, lo


def _split3(x):
    hi = x.astype(BF16)
    r = x - hi.astype(F32)
    mid = r.astype(BF16)
    lo = (r - mid.astype(F32)).astype(BF16)
    return hi, mid, lo


def _sigmoid(x):
    return 1.0 / (1.0 + jnp.exp(-x))


def _log_sigmoid(x):
    return jnp.minimum(x, 0.0) - jnp.log(1.0 + jnp.exp(-jnp.abs(x)))


def _layer_norm(z, g, b):
    mu = jnp.mean(z, axis=-1, keepdims=True)
    zc = z - mu
    var = jnp.mean(zc * zc, axis=-1, keepdims=True)
    return zc * lax.rsqrt(var + LN_EPS) * g + b


def _iota(shape, dim):
    return lax.broadcasted_iota(jnp.int32, shape, dim)


def _group_ones(n, group):
    return jnp.where(_iota((n, n), 0) // group == _iota((n, n), 1) // group, 1.0, 0.0).astype(BF16)


def _ada_body(c_ref, w_ref, b_ref, o_ref):
    x = c_ref[...]
    x = x * _sigmoid(x)
    xh, xl = _split2(x)
    wh, wl = _split2(w_ref[...])
    o_ref[...] = _dot(xh, wh) + _dot(xh, wl) + _dot(xl, wh) + b_ref[...]


def _ada(cvec, ada_w, ada_b):
    depth, d, n = ada_w.shape
    tn = 1024
    return pl.pallas_call(
        _ada_body,
        grid=(depth, n // tn),
        in_specs=[pl.BlockSpec((MOD_ROWS, d), lambda l, j: (0, 0)),
                  pl.BlockSpec((None, d, tn), lambda l, j: (l, 0, j)),
                  pl.BlockSpec((None, 1, tn), lambda l, j: (l, 0, j))],
        out_specs=pl.BlockSpec((None, MOD_ROWS, tn), lambda l, j: (l, 0, j)),
        out_shape=jax.ShapeDtypeStruct((depth, MOD_ROWS, n), F32),
        compiler_params=_cp("parallel", "parallel"),
        name="ada_mod",
    )(cvec, ada_w, ada_b.reshape(depth, 1, n))


def _col_tiles(w, tn):
    k, n = w.shape
    return w.reshape(k, n // tn, tn).transpose(1, 0, 2)


def _mod_spec(d, which, row_fn):
    return pl.BlockSpec((None, 1, d), lambda *g: (row_fn(*g), 0, which))


def _proj_body(ctx_tile, x_ref, c_ref, sc_ref, sh_ref, w_ref, ws_ref, o_ref, os_ref, u_ref):
    j = pl.program_id(2)
    last = pl.num_programs(2) - 1

    @pl.when(j == 0)
    def _():
        h = jnp.where(pl.program_id(1) == ctx_tile, c_ref[...], x_ref[...])
        u_ref[...] = (h * (1.0 + sc_ref[...]) + sh_ref[...]).astype(BF16)

    @pl.when(j < last)
    def _():
        o_ref[...] = _dot(u_ref[...], w_ref[...]).astype(o_ref.dtype)

    @pl.when(j == last)
    def _():
        os_ref[...] = _dot(u_ref[...], ws_ref[...])


def _proj(x, ctx_pad, mod, w, w_small, tn, batch, name):
    b, s_len, d = x.shape
    ctx_tile = s_len // TM
    tt = s_len + TM
    n = w.shape[1]
    ns = w_small.shape[1]
    nj = n // tn
    row = lambda bi, i, j: jnp.where(i == ctx_tile, batch, bi)
    return pl.pallas_call(
        functools.partial(_proj_body, ctx_tile),
        grid=(b, tt // TM, nj + 1),
        in_specs=[pl.BlockSpec((None, TM, d), lambda bi, i, j: (bi, jnp.minimum(i, ctx_tile - 1), 0)),
                  pl.BlockSpec((None, TM, d), lambda bi, i, j: (bi, 0, 0)),
                  _mod_spec(d, 1, row), _mod_spec(d, 0, row),
                  pl.BlockSpec((None, d, tn), lambda bi, i, j: (jnp.minimum(j, nj - 1), 0, 0)),
                  pl.BlockSpec((d, ns), lambda bi, i, j: (0, 0))],
        out_specs=[pl.BlockSpec((None, TM, tn), lambda bi, i, j: (bi, i, jnp.minimum(j, nj - 1))),
                   pl.BlockSpec((None, TM, ns), lambda bi, i, j: (bi, i, 0))],
        out_shape=[jax.ShapeDtypeStruct((b, tt, n), BF16), jax.ShapeDtypeStruct((b, tt, ns), F32)],
        scratch_shapes=[pltpu.VMEM((TM, d), BF16)],
        compiler_params=_cp("parallel", "parallel", "arbitrary"),
        name=name,
    )(x, ctx_pad, mod, mod, _col_tiles(w, tn), w_small)


def _seq_block(c, direction, n_lat, n_ctx):
    if direction == 0:
        return jnp.where(c < n_ctx, n_lat + c, c - n_ctx)
    return jnp.where(c < n_ctx, n_lat + n_ctx - 1 - c, n_lat - 1 - (c - n_ctx))


def _mlstm_body(direction, n_lat, n_ctx,
                qk_ref, qkp_ref, qkn_ref, v_ref, g_ref, gt_ref, cw_ref, gb_ref, gbt_ref,
                o_ref, ct_ref, n_ref, m_ref):
    L = M_CHUNK
    c = pl.program_id(1)
    n_live = n_lat + n_ctx

    @pl.when(c == 0)
    def _():
        ct_ref[...] = jnp.zeros_like(ct_ref)
        n_ref[...] = jnp.zeros_like(n_ref)
        m_ref[...] = jnp.zeros_like(m_ref)

    @pl.when(c >= n_live)
    def _():
        o_ref[...] = jnp.zeros_like(o_ref)

    @pl.when(c < n_live)
    def _():
        blk = _seq_block(c, direction, n_lat, n_ctx)
        first = jnp.logical_or(blk == 0, blk == n_lat)
        last = jnp.logical_or(blk == n_lat - 1, blk == n_live - 1)
        p = qk_ref[...].astype(F32)
        rows = _iota((L, 1), 0)
        prev_row = jnp.where(first, 0.0, qkp_ref[HALO - 1:HALO, :].astype(F32))
        next_row = jnp.where(last, 0.0, qkn_ref[0:1, :].astype(F32))
        pm = jnp.where(rows == 0, prev_row, pltpu.roll(p, 1, 0))
        pp = jnp.where(rows == L - 1, next_row, pltpu.roll(p, L - 1, 0))
        qkc = cw_ref[0:1, :] * pm + cw_ref[1:2, :] * p + cw_ref[2:3, :] * pp
        qk = qkc * _sigmoid(qkc)
        nq = M_HEADS * M_DQK
        q_all = qk[:, :nq]
        k_all = qk[:, nq:] * (M_DQK ** -0.5)
        v_all = v_ref[...]

        g = g_ref[...] + gb_ref[...]
        gt = gt_ref[...] + gbt_ref[...]
        r_i = _iota((L, L), 0)
        c_i = _iota((L, L), 1)
        causal = (r_i >= c_i) if direction == 0 else (r_i <= c_i)
        tri = jnp.where(causal, 1.0, 0.0).astype(BF16)
        tri_t = jnp.where(causal, 0.0, 1.0).astype(BF16) + jnp.where(r_i == c_i, 1.0, 0.0).astype(BF16)
        lf = _log_sigmoid(g)
        lft = _log_sigmoid(gt)
        b_col = sum(_dot(tri, part) for part in _split3(lf))
        b_row = sum(_dot(part, tri_t) for part in _split3(lft))
        e_last = L - 1 if direction == 0 else 0

        hs = []
        for hd in range(M_HEADS):
            ci = 16 * direction + hd
            cf = ci + 8
            q_h = q_all[:, hd * M_DQK:(hd + 1) * M_DQK]
            q_b = q_h.astype(BF16)
            k_b = k_all[:, hd * M_DQK:(hd + 1) * M_DQK].astype(BF16)
            hs.append(dict(b_c=b_col[:, cf:cf + 1], i_c=g[:, ci:ci + 1], b_r=b_row[cf:cf + 1, :],
                           i_r=gt[ci:ci + 1, :], m_st=m_ref[hd][0:1, 0:1], n_st=n_ref[hd][0:1, :],
                           q_h=q_h, q_b=q_b, k_b=k_b, v_h=v_all[:, hd * M_DV:(hd + 1) * M_DV].astype(F32),
                           qk=_dot_nt(q_b, k_b), qc=_dot(q_b, ct_ref[hd].astype(BF16))))
        for s in hs:
            dmat = jnp.where(causal, s["b_c"] - s["b_r"] + s["i_r"], NEG)
            inter = s["b_c"] + s["m_st"]
            s["m_t"] = jnp.maximum(inter, jnp.max(dmat, axis=-1, keepdims=True))
            s["s"] = s.pop("qk") * jnp.exp(dmat - s["m_t"])
            s["w_inter"] = jnp.exp(inter - s["m_t"])
        for s in hs:
            s["sv"] = _dot(s["s"].astype(BF16), s["v_h"].astype(BF16))
        for hd, s in enumerate(hs):
            num = s.pop("sv") + s["w_inter"] * s.pop("qc")
            den = (jnp.sum(s.pop("s"), axis=-1, keepdims=True)
                   + s.pop("w_inter") * jnp.sum(s.pop("q_h") * s["n_st"], axis=-1, keepdims=True))
            o_ref[:, hd * M_DV:(hd + 1) * M_DV] = num / jnp.maximum(jnp.abs(den), jnp.exp(-s.pop("m_t")))
        for s in hs:
            b_last = s["b_c"][e_last:e_last + 1, :]
            g_c = b_last - s["b_c"] + s["i_c"]
            g_r = b_last - s["b_r"] + s["i_r"]
            s["m_new"] = jnp.maximum(b_last + s["m_st"], jnp.max(g_r, axis=-1, keepdims=True))
            s["decay"] = jnp.exp(b_last + s["m_st"] - s["m_new"])
            s["kv"] = _dot_tn(s["k_b"], (jnp.exp(g_c - s["m_new"]) * s["v_h"]).astype(BF16))
            s["kn"] = _dot(jnp.broadcast_to(jnp.exp(g_r - s["m_new"]), (8, L)).astype(BF16), s["k_b"])[0:1, :]
        for hd, s in enumerate(hs):
            ct_ref[hd] = s["decay"] * ct_ref[hd] + s["kv"]
            n_ref[hd] = jnp.broadcast_to(s["decay"] * s["n_st"] + s["kn"], (8, M_DQK))
            m_ref[hd] = jnp.broadcast_to(s["m_new"], (8, 128))


def _mlstm_scan(p, gates, gates_t, conv_w, gb_row, gb_col, direction, s_len, c_len):
    b, tt, _ = p.shape
    L = M_CHUNK
    n_lat, n_ctx = s_len // L, c_len // L
    n_all = tt // L
    nqk = 2 * M_HEADS * M_DQK
    nv = M_HEADS * M_DV

    def blk(c):
        return jnp.where(c < n_lat + n_ctx, _seq_block(c, direction, n_lat, n_ctx), c)

    rh = L // HALO
    return pl.pallas_call(
        functools.partial(_mlstm_body, direction, n_lat, n_ctx),
        grid=(b, n_all),
        in_specs=[pl.BlockSpec((None, L, nqk), lambda bi, c: (bi, blk(c), 0)),
                  pl.BlockSpec((None, HALO, nqk), lambda bi, c: (bi, jnp.maximum(blk(c) * rh - 1, 0), 0)),
                  pl.BlockSpec((None, HALO, nqk),
                               lambda bi, c: (bi, jnp.minimum((blk(c) + 1) * rh, tt // HALO - 1), 0)),
                  pl.BlockSpec((None, L, nv), lambda bi, c: (bi, blk(c), 1)),
                  pl.BlockSpec((None, L, 128), lambda bi, c: (bi, blk(c), 0)),
                  pl.BlockSpec((None, 32, L), lambda bi, c: (bi, 0, blk(c))),
                  pl.BlockSpec((3, nqk), lambda bi, c: (0, 0)),
                  pl.BlockSpec((1, 128), lambda bi, c: (0, 0)),
                  pl.BlockSpec((32, 1), lambda bi, c: (0, 0))],
        out_specs=pl.BlockSpec((None, L, nv), lambda bi, c: (bi, blk(c), 0)),
        out_shape=jax.ShapeDtypeStruct((b, tt, nv), F32),
        scratch_shapes=[pltpu.VMEM((M_HEADS, M_DQK, M_DV), F32),
                        pltpu.VMEM((M_HEADS, 8, M_DQK), F32),
                        pltpu.VMEM((M_HEADS, 8, 128), F32)],
        compiler_params=_cp("parallel", "arbitrary"),
        name="mlstm_scan_%d" % direction,
    )(p, p, p, p, gates, gates_t, conv_w, gb_row, gb_col)


def _out_epilogue(z_b, w_ref, res, gate_ref, lg_ref, lb_ref, o_ref):
    y = _dot(z_b, w_ref[...])
    o_ref[...] = _layer_norm(ALPHA * res + gate_ref[...] * y, lg_ref[...], lb_ref[...])


def _mlstm_out_body(s_len, hf_ref, hb_ref, og_ref, nw_ref, w_ref, resx_ref, resc_ref, gate_ref, lg_ref, lb_ref,
                    o_ref, z_ref):
    ones = jnp.full((M_DV, M_DV), 1.0, BF16)
    for hd in range(M_HEADS):
        sl = slice(hd * M_DV, (hd + 1) * M_DV)
        hh = hf_ref[:, sl] + hb_ref[:, sl]
        mu = _dot(hh.astype(BF16), ones) * (1.0 / M_DV)
        hc = hh - mu
        var = _dot((hc * hc).astype(BF16), ones) * (1.0 / M_DV)
        hn = hc * lax.rsqrt(var + M_NORM_EPS) * nw_ref[:, sl]
        z_ref[:, sl] = (_sigmoid(og_ref[:, sl].astype(F32)) * hn).astype(BF16)
    res = jnp.where(pl.program_id(1) * TMO >= s_len, resc_ref[...], resx_ref[...])
    _out_epilogue(z_ref[...], w_ref, res, gate_ref, lg_ref, lb_ref, o_ref)


def _rwkv_out_body(yf_ref, yb_ref, y0_ref, bon_ref, gt_ref, lw_ref, lbx_ref,
                   w_ref, res_ref, gate_ref, lg_ref, lb_ref, o_ref, z_ref):
    ones = _group_ones(R_GROUP, R_HEAD)
    d = o_ref.shape[-1]
    for gi in range(d // R_GROUP):
        sl = slice(gi * R_GROUP, (gi + 1) * R_GROUP)
        y = yf_ref[:, sl].astype(F32) + yb_ref[:, sl].astype(F32) + y0_ref[:, sl]
        mu = _dot(y.astype(BF16), ones) * (1.0 / R_HEAD)
        yc = y - mu
        var = _dot((yc * yc).astype(BF16), ones) * (1.0 / R_HEAD)
        yn = yc * lax.rsqrt(var + R_GN_EPS) * lw_ref[:, sl] + lbx_ref[:, sl]
        z_ref[:, sl] = ((yn + bon_ref[:, sl]) * gt_ref[:, sl]).astype(BF16)
    _out_epilogue(z_ref[...], w_ref, res_ref[...], gate_ref, lg_ref, lb_ref, o_ref)


def _out_call(body, streams, stream_col, vecs, w, res, mod, ln_g, ln_b, batch, s_len, n_tiles, name):
    res = res if isinstance(res, tuple) else (res,)
    b, _, d = res[0].shape
    row = lambda bi, i: jnp.where(i * TMO >= s_len, batch, bi)
    tile = lambda col: pl.BlockSpec((None, TMO, d), lambda bi, i: (bi, i, col))
    vec = pl.BlockSpec((1, d), lambda bi, i: (0, 0))
    n_lat, n_ctx = s_len // TMO, TM // TMO
    res_specs = [tile(0)] if len(res) == 1 else [
        pl.BlockSpec((None, TMO, d), lambda bi, i: (bi, jnp.minimum(i, n_lat - 1), 0)),
        pl.BlockSpec((None, TMO, d), lambda bi, i: (bi, jnp.clip(i - n_lat, 0, n_ctx - 1), 0))]
    return pl.pallas_call(
        body,
        grid=(b, n_tiles),
        in_specs=[tile(col) for col in stream_col] + [vec] * len(vecs)
                 + [pl.BlockSpec((d, d), lambda bi, i: (0, 0))] + res_specs + [_mod_spec(d, 2, row), vec, vec],
        out_specs=tile(0),
        out_shape=jax.ShapeDtypeStruct((b, n_tiles * TMO, d), F32),
        scratch_shapes=[pltpu.VMEM((TMO, d), BF16)],
        compiler_params=_cp("parallel", "parallel"),
        name=name,
    )(*streams, *vecs, w, *res, mod, ln_g, ln_b)


def _mlp_body(h_ref, sc_ref, sh_ref, gate_ref, w1_ref, w2_ref, lg_ref, lb_ref, o_ref, u_ref, acc_ref):
    k = pl.program_id(2)

    @pl.when(k == 0)
    def _():
        u_ref[...] = (h_ref[...] * (1.0 + sc_ref[...]) + sh_ref[...]).astype(BF16)
        acc_ref[...] = jnp.zeros_like(acc_ref)

    a = jnp.maximum(_dot(u_ref[...], w1_ref[...]), 0.0)
    acc_ref[...] += _dot((a * a).astype(BF16), w2_ref[...])

    @pl.when(k == pl.num_programs(2) - 1)
    def _():
        o_ref[...] = _layer_norm(ALPHA * h_ref[...] + gate_ref[...] * acc_ref[...], lg_ref[...], lb_ref[...])


def _mlp(h, mod, w1, w2, ln_g, ln_b, batch, ctx_tile, n_tiles, name):
    b, _, d = h.shape
    dff = w1.shape[1]
    row = lambda bi, i, k: jnp.where(i == ctx_tile, batch, bi)
    tile = pl.BlockSpec((None, TM, d), lambda bi, i, k: (bi, i, 0))
    vec = pl.BlockSpec((1, d), lambda bi, i, k: (0, 0))
    return pl.pallas_call(
        _mlp_body,
        grid=(b, n_tiles, dff // FF_TILE),
        in_specs=[tile, _mod_spec(d, 4, row), _mod_spec(d, 3, row), _mod_spec(d, 5, row),
                  pl.BlockSpec((None, d, FF_TILE), lambda bi, i, k: (k, 0, 0)),
                  pl.BlockSpec((FF_TILE, d), lambda bi, i, k: (k, 0)), vec, vec],
        out_specs=tile,
        out_shape=jax.ShapeDtypeStruct((b, n_tiles * TM, d), F32),
        scratch_shapes=[pltpu.VMEM((TM, d), BF16), pltpu.VMEM((TM, d), F32)],
        compiler_params=_cp("parallel", "parallel", "arbitrary"),
        name=name,
    )(h, mod, mod, mod, _col_tiles(w1, FF_TILE), w2, ln_g, ln_b)


def _rproj_body(ctx_tile, c_len, n_lat_tiles,
                h_ref, hp_ref, hn_ref, sc_ref, sh_ref, mu_ref, w_ref, wl_ref, o_ref, ol_ref, u_ref, xx_ref):
    i = pl.program_id(1)

    @pl.when(pl.program_id(2) == 0)
    def _():
        is_ctx = i == ctx_tile
        sc = 1.0 + sc_ref[...]
        sh = sh_ref[...]
        u = h_ref[...] * sc + sh
        u_prev = hp_ref[...] * sc + sh
        u_next = hn_ref[...] * sc + sh
        u_ref[...] = u
        t = _iota((TM, 1), 0)
        tw = jnp.bitwise_and(t, GRID_W - 1)
        q = u.shape[1] // 4
        flag = lambda cond: jnp.where(cond, 1, 0)
        ctx_prev, ctx_next = flag(t != 0), flag(t != c_len - 1)
        lat_left, lat_right = flag(tw != 0), flag(tw != GRID_W - 1)
        lat_up = jnp.where(i == 0, flag(t >= GRID_W), 1)
        lat_down = jnp.where(i == n_lat_tiles - 1, flag(t < TM - GRID_W), 1)
        for qi in range(4):
            sl = slice(qi * q, (qi + 1) * q)
            uq = u[:, sl]
            before = pltpu.roll(uq, 1, 0)
            after = pltpu.roll(uq, TM - 1, 0)
            if qi == 0:
                src = before
                keep = jnp.where(is_ctx, ctx_prev, lat_left)
            elif qi == 1:
                src = jnp.where(is_ctx, before, after)
                keep = jnp.where(is_ctx, ctx_prev, lat_right)
            elif qi == 2:
                above = jnp.concatenate([u_prev[:, sl], uq[:TM - GRID_W]], axis=0)
                src = jnp.where(is_ctx, after, above)
                keep = jnp.where(is_ctx, ctx_next, lat_up)
            else:
                below = jnp.concatenate([uq[GRID_W:], u_next[:, sl]], axis=0)
                src = jnp.where(is_ctx, after, below)
                keep = jnp.where(is_ctx, ctx_next, lat_down)
            xx_ref[:, sl] = jnp.where(keep != 0, src, 0.0) - uq

    j = pl.program_id(2)
    last = pl.num_programs(2) - 1

    @pl.when(j < last)
    def _():
        x = (u_ref[...] + xx_ref[...] * mu_ref[j]).astype(BF16)
        o_ref[...] = _dot(x, w_ref[...]).astype(o_ref.dtype)

    @pl.when(j == last)
    def _():
        for ti in range(ol_ref.shape[-1] // R_LORA_PAD):
            sl = slice(ti * R_LORA_PAD, (ti + 1) * R_LORA_PAD)
            x = (u_ref[...] + xx_ref[...] * mu_ref[3 + max(ti - 1, 0)]).astype(BF16)
            ol_ref[:, sl] = _dot(x, wl_ref[:, sl])


def _rproj(h, mod, mu, w, w_lora, batch, s_len, c_len):
    b, tt, d = h.shape
    n = w.shape[1]
    nl = w_lora.shape[1]
    nj = n // d
    ctx_tile = s_len // TM
    row = lambda bi, i, j: jnp.where(i == ctx_tile, batch, bi)
    r64 = TM // GRID_W
    return pl.pallas_call(
        functools.partial(_rproj_body, ctx_tile, c_len, s_len // TM),
        grid=(b, tt // TM, nj + 1),
        in_specs=[pl.BlockSpec((None, TM, d), lambda bi, i, j: (bi, i, 0)),
                  pl.BlockSpec((None, GRID_W, d), lambda bi, i, j: (bi, jnp.maximum(i * r64 - 1, 0), 0)),
                  pl.BlockSpec((None, GRID_W, d),
                               lambda bi, i, j: (bi, jnp.minimum((i + 1) * r64, tt // GRID_W - 1), 0)),
                  _mod_spec(d, 1, row), _mod_spec(d, 0, row),
                  pl.BlockSpec((8, 1, d), lambda bi, i, j: (0, 0, 0)),
                  pl.BlockSpec((None, d, d), lambda bi, i, j: (jnp.minimum(j, nj - 1), 0, 0)),
                  pl.BlockSpec((d, nl), lambda bi, i, j: (0, 0))],
        out_specs=[pl.BlockSpec((None, TM, d), lambda bi, i, j: (bi, i, jnp.minimum(j, nj - 1))),
                   pl.BlockSpec((None, TM, nl), lambda bi, i, j: (bi, i, 0))],
        out_shape=[jax.ShapeDtypeStruct((b, tt, n), BF16), jax.ShapeDtypeStruct((b, tt, nl), F32)],
        scratch_shapes=[pltpu.VMEM((TM, d), F32), pltpu.VMEM((TM, d), F32)],
        compiler_params=_cp("parallel", "parallel", "arbitrary"),
        name="rwkv_proj",
    )(h, h, h, mod, mod, mu, _col_tiles(w, d), w_lora)


def _block_diag(x, mask):
    return jnp.where(mask, jnp.concatenate([x] * (R_GROUP // R_CHUNK), axis=0), 0.0)


def _rloc_body(r_ref, k_ref, v_ref, lo_ref, g2_ref, w2_ref, a2_ref, vec_ref,
               rf_ref, rb_ref, y0_ref, bon_ref, gate_ref, gf_ref, gb_ref, hf_ref, hb_ref, dgf_ref, dgb_ref,
               rs_ref, vs_ref, kk_ref, lw_ref, kd_ref, bb_ref):
    L = R_CHUNK
    W = R_GROUP
    lo = lo_ref[...]
    r = r_ref[...].astype(F32)
    k = k_ref[...].astype(F32)
    v = v_ref[...].astype(F32)
    w0 = (vec_ref[0:1, :], vec_ref[1:2, :])
    a0 = (vec_ref[2:3, :], vec_ref[3:4, :])
    k_k = vec_ref[4:5, :]
    k_a = vec_ref[5:6, :]
    r_k = vec_ref[6:7, :]
    ones = _group_ones(W, R_HEAD)

    gate_ref[...] = _dot(_sigmoid(lo[:, :R_GATE_LORA]).astype(BF16), g2_ref[...])
    kraw = k * k_k
    ss = _dot((kraw * kraw).astype(BF16), ones)
    kk = kraw / jnp.maximum(jnp.sqrt(ss), 1e-12)
    rs_ref[...] = r
    vs_ref[...] = v
    kk_ref[...] = kk
    kd_sum = jnp.zeros_like(k)
    for d in range(2):
        wl = lo[:, R_GATE_LORA + d * R_LORA_PAD:R_GATE_LORA + (d + 1) * R_LORA_PAD]
        al = lo[:, R_GATE_LORA + (2 + d) * R_LORA_PAD:R_GATE_LORA + (3 + d) * R_LORA_PAD]
        lwx = w0[d] + _dot(jnp.tanh(wl).astype(BF16), w2_ref[d])
        lw_ref[d] = -_sigmoid(lwx) * (jnp.e ** -0.5)
        iclr = _sigmoid(a0[d] + _dot(al.astype(BF16), a2_ref[d]))
        kd = k * (1.0 + (iclr - 1.0) * k_a)
        kd_ref[d] = kd
        bb_ref[d] = kk * iclr
        kd_sum = kd_sum + kd
    bon_ref[...] = _dot((r * r_k * kd_sum).astype(BF16), ones) * v

    lane_head = _iota((W, W), 1) // R_HEAD
    bd_mask = _iota((W, W), 0) // L == lane_head
    t_i = _iota((L, W), 0)
    s_i = jnp.bitwise_and(_iota((L, W), 1), L - 1)
    eye = jnp.where(t_i == s_i, 1.0, 0.0)
    ts_xor = jnp.bitwise_xor(t_i, s_i)
    diag2 = jnp.right_shift(ts_xor, 1) == 0
    off_masks = [jnp.right_shift(ts_xor, lv) == 1 for lv in range(1, 6)]
    head_of_lane = _iota((L, W), 1) // R_HEAD
    r_i = _iota((L, L), 0)
    c_i = _iota((L, L), 1)
    outs = ((rf_ref, gf_ref, hf_ref), (rb_ref, gb_ref, hb_ref))

    tris = (jnp.where(r_i >= c_i, 1.0, 0.0).astype(BF16), jnp.where(r_i <= c_i, 1.0, 0.0).astype(BF16))
    stricts = (t_i > s_i, t_i < s_i)
    incls = (t_i >= s_i, t_i <= s_i)
    bd = lambda x: _block_diag(x, bd_mask).astype(BF16)

    def head_t(x):
        xt = x.T
        return jnp.concatenate([xt[hd * R_HEAD:(hd + 1) * R_HEAD, :] for hd in range(W // R_HEAD)], axis=1)

    n_chunks = R_BLOCK // L
    inst = [(ci, d) for ci in range(n_chunks) for d in range(2)]
    rows_of = [pl.ds(ci * L, L) for ci in range(n_chunks)]
    vcs = [vs_ref[rows_of[ci], :] for ci in range(n_chunks)]
    bd_vs = [bd(vc) for vc in vcs]

    st = []
    for ci, d in inst:
        rows = rows_of[ci]
        lwc = lw_ref[d, rows, :]
        cum = sum(_dot(tris[d], part) for part in _split3(lwc))
        e_last = L - 1 if d == 0 else 0
        tot = cum[e_last:e_last + 1, :]
        p_inv = jnp.exp(-cum)
        p_end = jnp.exp(tot - cum)
        bc = bb_ref[d, rows, :]
        kdc = kd_ref[d, rows, :]
        st.append(dict(tot=tot, a_t=-kk_ref[rows, :] * jnp.exp(cum - lwc), r_t=rs_ref[rows, :] * jnp.exp(cum),
                       b_inv=bc * p_inv, k_inv=kdc * p_inv,
                       b_end=head_t(bc * p_end).astype(BF16), k_end=head_t(kdc * p_end).astype(BF16)))
    for s, (ci, d) in zip(st, inst):
        z = _dot_nt(jnp.concatenate([s["a_t"], s["r_t"]], axis=0).astype(BF16),
                    jnp.concatenate([bd(s.pop("b_inv")), bd(s.pop("k_inv"))], axis=0))
        s["a_ab"] = jnp.where(stricts[d], z[:L, :W], 0.0)
        s["a_ak"] = jnp.where(stricts[d], z[:L, W:], 0.0).astype(BF16)
        s["m_rb"] = jnp.where(incls[d], z[L:, :W], 0.0).astype(BF16)
        s["m_rk"] = jnp.where(incls[d], z[L:, W:], 0.0).astype(BF16)
        s["tinv"] = eye + jnp.where(diag2, s["a_ab"], 0.0)
    for ci in range(n_chunks):
        pair = [s for s, (cj, d) in zip(st, inst) if cj == ci]
        lhs = [s.pop("a_ak") for s in pair] + [s.pop("m_rk") for s in pair] + [s.pop("k_end") for s in pair]
        xv = _dot(jnp.concatenate(lhs, axis=0), bd_vs[ci])
        for d, s in enumerate(pair):
            s["akv"] = xv[d * L:(d + 1) * L]
            s["mkv"] = xv[(2 + d) * L:(3 + d) * L]
            s["zk"] = xv[(4 + d) * L:(5 + d) * L]
    for off in off_masks:
        for s in st:
            s["x1"] = _dot(s["tinv"].astype(BF16), bd(jnp.where(off, s["a_ab"], 0.0)))
        for s in st:
            s["tinv"] = s["tinv"] + _dot(s.pop("x1").astype(BF16), bd(s["tinv"]))
    for s in st:
        x = _dot(s.pop("tinv").astype(BF16), jnp.concatenate([bd(s.pop("a_t")), bd(s.pop("akv"))], axis=1))
        s["a_p"] = x[:, :W]
        s["u0"] = x[:, W:]
    y0s = [jnp.zeros((L, W), F32) for _ in range(n_chunks)]
    for s, (ci, d) in zip(st, inst):
        yz = _dot(jnp.concatenate([s.pop("m_rb"), s.pop("b_end")], axis=0),
                  jnp.concatenate([bd(s.pop("a_p")), bd(s.pop("u0"))], axis=1))
        outs[d][0][rows_of[ci], :] = (s.pop("r_t") + yz[:L, :W]).astype(BF16)
        y0s[ci] = y0s[ci] + yz[:L, W:] + s.pop("mkv")
        outs[d][1][rows_of[ci], :] = yz[L:, :W].astype(BF16)
        outs[d][2][rows_of[ci], :] = (yz[L:, W:] + s.pop("zk")).astype(BF16)
    for ci in range(n_chunks):
        y0_ref[rows_of[ci], :] = y0s[ci]
    for dg_ref in (dgf_ref, dgb_ref):
        dg_ref[n_chunks:, :] = jnp.zeros((8 - n_chunks, W), F32)
    for s, (ci, d) in zip(st, inst):
        (dgf_ref, dgb_ref)[d][ci:ci + 1, :] = jnp.exp(s["tot"])


def _rloc(rkv, lora, g2, w2p, a2p, vecs, s_len, c_len):
    b, tt, n3 = rkv.shape
    d = n3 // 3
    nb = (s_len + c_len) // R_BLOCK
    ng = d // R_GROUP
    tok = lambda col: pl.BlockSpec((None, R_BLOCK, R_GROUP), lambda bi, i, j: (bi, i, col * ng + j))
    out_tok = pl.BlockSpec((None, R_BLOCK, R_GROUP), lambda bi, i, j: (bi, i, j))
    f32_out = jax.ShapeDtypeStruct((b, tt, d), F32)
    bf_out = jax.ShapeDtypeStruct((b, tt, d), BF16)
    sc = lambda *shape: pltpu.VMEM(shape, F32)
    return pl.pallas_call(
        _rloc_body,
        grid=(b, nb, ng),
        in_specs=[tok(0), tok(1), tok(2),
                  pl.BlockSpec((None, R_BLOCK, lora.shape[2]), lambda bi, i, j: (bi, i, 0)),
                  pl.BlockSpec((R_GATE_LORA, R_GROUP), lambda bi, i, j: (0, j)),
                  pl.BlockSpec((2, R_LORA_PAD, R_GROUP), lambda bi, i, j: (0, 0, j)),
                  pl.BlockSpec((2, R_LORA_PAD, R_GROUP), lambda bi, i, j: (0, 0, j)),
                  pl.BlockSpec((8, R_GROUP), lambda bi, i, j: (0, j))],
        out_specs=[out_tok] * 9 + [pl.BlockSpec((None, None, 8, R_GROUP), lambda bi, i, j: (bi, i, 0, j))] * 2,
        out_shape=[bf_out, bf_out, f32_out, f32_out, f32_out, bf_out, bf_out, bf_out, bf_out,
                   jax.ShapeDtypeStruct((b, nb, 8, d), F32), jax.ShapeDtypeStruct((b, nb, 8, d), F32)],
        scratch_shapes=[sc(R_BLOCK, R_GROUP), sc(R_BLOCK, R_GROUP), sc(R_BLOCK, R_GROUP),
                        sc(2, R_BLOCK, R_GROUP), sc(2, R_BLOCK, R_GROUP), sc(2, R_BLOCK, R_GROUP)],
        compiler_params=_cp("parallel", "parallel", "arbitrary"),
        name="rwkv_local",
    )(rkv, rkv, rkv, lora, g2, w2p, a2p, vecs)


def _rseq_body(n_lat, n_ctx, gf_ref, gb_ref, hf_ref, hb_ref, rf_ref, rb_ref, dgf_ref, dgb_ref,
               yf_ref, yb_ref, st_ref):
    L = R_CHUNK
    W = R_GROUP
    c = pl.program_id(1)

    @pl.when(c == 0)
    def _():
        st_ref[...] = jnp.zeros_like(st_ref)

    bd_mask = _iota((W, W), 0) // L == _iota((W, W), 1) // R_HEAD
    eye = jnp.where(_iota((L, W), 0) == jnp.bitwise_and(_iota((L, W), 1), L - 1), 1.0, 0.0)
    ins = ((gf_ref, hf_ref, rf_ref, dgf_ref, yf_ref), (gb_ref, hb_ref, rb_ref, dgb_ref, yb_ref))
    for d in range(2):
        g_ref, ha_ref, rp_ref, dg_ref, y_ref = ins[d]
        chunk_in_block = jnp.bitwise_and(_seq_block(c, d, n_lat, n_ctx), R_BLOCK // L - 1)
        for gi in range(st_ref.shape[-1] // W):
            sl = slice(gi * W, (gi + 1) * W)
            st = st_ref[d, :, sl]
            st_hi = st.astype(BF16)
            st_lo = (st - st_hi.astype(F32)).astype(BF16)
            g_hi, g_lo = _split2(g_ref[:, sl].astype(F32) + eye * dg_ref[pl.ds(chunk_in_block, 1), sl])
            rp = rp_ref[:, sl]
            bd_hi = _block_diag(st_hi.astype(F32), bd_mask).astype(BF16)
            bd_lo = _block_diag(st_lo.astype(F32), bd_mask).astype(BF16)
            o1 = _dot(jnp.concatenate([g_hi, g_lo, rp], axis=0), bd_hi)
            o2 = _dot(jnp.concatenate([g_hi, rp], axis=0), bd_lo)
            st_ref[d, :, sl] = o1[:L] + o1[L:2 * L] + o2[:L] + ha_ref[:, sl].astype(F32)
            y_ref[:, sl] = (o1[2 * L:] + o2[L:]).astype(y_ref.dtype)


def _rseq(gf, gb, hf, hb, rf, rb, dgf, dgb, s_len, c_len):
    b, tt, d = gf.shape
    L = R_CHUNK
    n_lat, n_ctx = s_len // L, c_len // L
    per_block = R_BLOCK // L
    spec = lambda direction: pl.BlockSpec(
        (None, L, d), lambda bi, c: (bi, _seq_block(c, direction, n_lat, n_ctx), 0))
    dspec = lambda direction: pl.BlockSpec(
        (None, None, 8, d), lambda bi, c: (bi, _seq_block(c, direction, n_lat, n_ctx) // per_block, 0, 0))
    out = jax.ShapeDtypeStruct((b, tt, d), BF16)
    return pl.pallas_call(
        functools.partial(_rseq_body, n_lat, n_ctx),
        grid=(b, n_lat + n_ctx),
        in_specs=[spec(0), spec(1), spec(0), spec(1), spec(0), spec(1), dspec(0), dspec(1)],
        out_specs=[spec(0), spec(1)],
        out_shape=[out, out],
        scratch_shapes=[pltpu.VMEM((2, L, d), F32)],
        compiler_params=_cp("parallel", "arbitrary"),
        name="rwkv_seq",
    )(gf, gb, hf, hb, rf, rb, dgf, dgb)


def kernel(x, c, ctx, c_ctx, ada_w, ada_b, ln_g, ln_b, mlp_w1, mlp_w2, m_w_in, m_conv, m_gate_b, m_norm_w, m_w_out, r_w_in, r_mu, r_g2, r_w0, r_w2, r_a0, r_a2, r_k_k, r_k_a, r_r_k, r_lnx_w, r_lnx_b, r_w_out):
    batch, s_len, d = x.shape
    c_len = ctx.shape[1]
    assert batch < MOD_ROWS and s_len % TM == 0 and c_len % M_CHUNK == 0 and c_len <= TM
    assert s_len % GRID_W == 0 and d % R_GROUP == 0 and ada_w.shape[0] == DEPTH == 2
    tt = s_len + TM
    ctx_tile = s_len // TM
    row = lambda a: a.reshape(1, -1)

    ctx_pad = jnp.concatenate([ctx, jnp.zeros((batch, TM - c_len, d), F32)], axis=1)
    cvec = jnp.concatenate([c, c_ctx[None], jnp.zeros((MOD_ROWS - batch - 1, d), F32)], axis=0)
    mod = _ada(cvec, ada_w, ada_b).reshape(DEPTH, MOD_ROWS, 1, 6 * d)

    nqk = 2 * M_HEADS * M_DQK
    nv = M_HEADS * M_DV
    w_in = m_w_in[0]
    w_gates = jnp.pad(w_in[:, nqk + 2 * nv:], ((0, 0), (0, 128 - 4 * M_HEADS))).astype(BF16)
    p, gates = _proj(x, ctx_pad, mod[0], w_in[:, :nqk + 2 * nv].astype(BF16), w_gates, 1024, batch, "mlstm_proj")
    gates_t = jnp.swapaxes(gates[:, :, :4 * M_HEADS], 1, 2)
    gb = m_gate_b[0].reshape(-1)
    gb_row = jnp.pad(gb, (0, 128 - gb.shape[0])).reshape(1, 128)
    gb_col = gb.reshape(-1, 1)
    scan = functools.partial(_mlstm_scan, p, gates, gates_t, m_conv[0], gb_row, gb_col)
    h_f = scan(0, s_len, c_len)
    h_b = scan(1, s_len, c_len)
    h = _out_call(functools.partial(_mlstm_out_body, s_len), (h_f, h_b, p), (0, 0, (nqk + nv) // d),
                  (row(m_norm_w[0]),), m_w_out[0].astype(BF16), (x, ctx_pad), mod[0], row(ln_g[0, 0]), row(ln_b[0, 0]),
                  batch, s_len, tt // TMO, "mlstm_out")
    h = _mlp(h, mod[0], mlp_w1[0].astype(BF16), mlp_w2[0].astype(BF16), row(ln_g[0, 1]), row(ln_b[0, 1]),
             batch, ctx_tile, tt // TM, "mlp0")

    w_in = r_w_in[0]
    mu = r_mu[0].reshape(8, 1, d)
    lo_cols = [w_in[:, 3 * d:3 * d + R_GATE_LORA]]
    for gi in range(4):
        start = 3 * d + R_GATE_LORA + gi * R_LORA
        lo_cols.append(jnp.pad(w_in[:, start:start + R_LORA], ((0, 0), (0, R_LORA_PAD - R_LORA))))
    w_lora = jnp.concatenate(lo_cols, axis=1).astype(BF16)
    rkv, lora = _rproj(h, mod[1], mu, w_in[:, :3 * d].astype(BF16), w_lora, batch, s_len, c_len)
    pad_rows = ((0, 0), (0, R_LORA_PAD - R_LORA), (0, 0))
    vecs = jnp.concatenate([r_w0[0], r_a0[0], row(r_k_k[0]), row(r_k_a[0]), row(r_r_k[0]),
                            jnp.zeros((1, d), F32)], axis=0)
    rf, rb, y0, bonus, gate, g_f, g_b, ha_f, ha_b, dg_f, dg_b = _rloc(
        rkv, lora, r_g2[0].astype(BF16), jnp.pad(r_w2[0], pad_rows).astype(BF16),
        jnp.pad(r_a2[0], pad_rows).astype(BF16), vecs, s_len, c_len)
    y_f, y_b = _rseq(g_f, g_b, ha_f, ha_b, rf, rb, dg_f, dg_b, s_len, c_len)
    h1 = _out_call(_rwkv_out_body, (y_f, y_b, y0, bonus, gate), (0, 0, 0, 0, 0),
                   (row(r_lnx_w[0]), row(r_lnx_b[0])), r_w_out[0].astype(BF16), h, mod[1],
                   row(ln_g[1, 0]), row(ln_b[1, 0]), batch, s_len, s_len // TMO, "rwkv_out")
    return _mlp(h1, mod[1], mlp_w1[1].astype(BF16), mlp_w2[1].astype(BF16), row(ln_g[1, 1]), row(ln_b[1, 1]),
                batch, ctx_tile, s_len // TM, "mlp1")
```

```python
import functools

import jax
import jax.numpy as jnp
from jax import lax
from jax.experimental import pallas as pl
from jax.experimental.pallas import tpu as pltpu

F32 = jnp.float32
BF16 = jnp.bfloat16

DEPTH = 2
GRID_W = 64
ALPHA = (2 * DEPTH) ** 0.25
LN_EPS = 1e-5
M_HEADS = 8
M_DQK = 128
M_DV = 256
M_NORM_EPS = 1e-6
R_HEAD = 64
R_GN_EPS = 64e-5
R_LORA = 96
R_LORA_PAD = 128
R_GATE_LORA = 256

TM = 512
TMO = 256
M_CHUNK = 256
HALO = 16
R_CHUNK = 64
R_BLOCK = 256
R_GROUP = 256
FF_TILE = 1024
MOD_ROWS = 8
NEG = -1e30
VMEM_LIMIT = 52 * 1024 * 1024


def _cp(*sem):
    return pltpu.CompilerParams(dimension_semantics=sem, vmem_limit_bytes=VMEM_LIMIT)


def _dot(a, b):
    return jnp.dot(a, b, preferred_element_type=F32)


def _dot_nt(a, b):
    return lax.dot_general(a, b, (((1,), (1,)), ((), ())), preferred_element_type=F32)


def _dot_tn(a, b):
    return lax.dot_general(a, b, (((0,), (0,)), ((), ())), preferred_element_type=F32)


def _split2(x):
    hi = x.astype(BF16)
    lo = (x - hi.astype(F32)).astype(BF16)
    return hi, lo


def _split3(x):
    hi = x.astype(BF16)
    r = x - hi.astype(F32)
    mid = r.astype(BF16)
    lo = (r - mid.astype(F32)).astype(BF16)
    return hi, mid, lo


def _sigmoid(x):
    return 1.0 / (1.0 + jnp.exp(-x))


def _log_sigmoid(x):
    return jnp.minimum(x, 0.0) - jnp.log(1.0 + jnp.exp(-jnp.abs(x)))


def _layer_norm(z, g, b):
    mu = jnp.mean(z, axis=-1, keepdims=True)
    zc = z - mu
    var = jnp.mean(zc * zc, axis=-1, keepdims=True)
    return zc * lax.rsqrt(var + LN_EPS) * g + b


def _iota(shape, dim):
    return lax.broadcasted_iota(jnp.int32, shape, dim)


def _group_ones(n, group):
    return jnp.where(_iota((n, n), 0) // group == _iota((n, n), 1) // group, 1.0, 0.0).astype(BF16)


def _ada_body(c_ref, w_ref, b_ref, o_ref):
    x = c_ref[...]
    x = x * _sigmoid(x)
    xh, xl = _split2(x)
    wh, wl = _split2(w_ref[...])
    o_ref[...] = _dot(xh, wh) + _dot(xh, wl) + _dot(xl, wh) + b_ref[...]


def _ada(cvec, ada_w, ada_b):
    depth, d, n = ada_w.shape
    tn = 1024
    return pl.pallas_call(
        _ada_body,
        grid=(depth, n // tn),
        in_specs=[pl.BlockSpec((MOD_ROWS, d), lambda l, j: (0, 0)),
                  pl.BlockSpec((None, d, tn), lambda l, j: (l, 0, j)),
                  pl.BlockSpec((None, 1, tn), lambda l, j: (l, 0, j))],
        out_specs=pl.BlockSpec((None, MOD_ROWS, tn), lambda l, j: (l, 0, j)),
        out_shape=jax.ShapeDtypeStruct((depth, MOD_ROWS, n), F32),
        compiler_params=_cp("parallel", "parallel"),
        name="ada_mod",
    )(cvec, ada_w, ada_b.reshape(depth, 1, n))


def _mod_spec(d, which, row_fn):
    return pl.BlockSpec((None, 1, d), lambda *g: (row_fn(*g), 0, which))


def _proj_body(ctx_tile, x_ref, c_ref, sc_ref, sh_ref, w_ref, ws_ref, o_ref, os_ref, u_ref):
    @pl.when(pl.program_id(2) == 0)
    def _():
        h = jnp.where(pl.program_id(1) == ctx_tile, c_ref[...], x_ref[...])
        u_ref[...] = (h * (1.0 + sc_ref[...]) + sh_ref[...]).astype(BF16)
        os_ref[...] = _dot(u_ref[...], ws_ref[...])

    o_ref[...] = _dot(u_ref[...], w_ref[...]).astype(o_ref.dtype)


def _proj(x, ctx_pad, mod, w, w_small, tn, batch, name):
    b, s_len, d = x.shape
    ctx_tile = s_len // TM
    tt = s_len + TM
    n = w.shape[1]
    ns = w_small.shape[1]
    nj = n // tn
    row = lambda bi, i, j: jnp.where(i == ctx_tile, batch, bi)
    return pl.pallas_call(
        functools.partial(_proj_body, ctx_tile),
        grid=(b, tt // TM, nj),
        in_specs=[pl.BlockSpec((None, TM, d), lambda bi, i, j: (bi, jnp.minimum(i, ctx_tile - 1), 0)),
                  pl.BlockSpec((None, TM, d), lambda bi, i, j: (bi, 0, 0)),
                  _mod_spec(d, 1, row), _mod_spec(d, 0, row),
                  pl.BlockSpec((d, tn), lambda bi, i, j: (0, j)),
                  pl.BlockSpec((d, ns), lambda bi, i, j: (0, 0))],
        out_specs=[pl.BlockSpec((None, TM, tn), lambda bi, i, j: (bi, i, j)),
                   pl.BlockSpec((None, TM, ns), lambda bi, i, j: (bi, i, 0))],
        out_shape=[jax.ShapeDtypeStruct((b, tt, n), BF16), jax.ShapeDtypeStruct((b, tt, ns), F32)],
        scratch_shapes=[pltpu.VMEM((TM, d), BF16)],
        compiler_params=_cp("parallel", "parallel", "arbitrary"),
        name=name,
    )(x, ctx_pad, mod, mod, w, w_small)


def _seq_block(c, direction, n_lat, n_ctx):
    if direction == 0:
        return jnp.where(c < n_ctx, n_lat + c, c - n_ctx)
    return jnp.where(c < n_ctx, n_lat + n_ctx - 1 - c, n_lat - 1 - (c - n_ctx))


def _mlstm_body(direction, n_lat, n_ctx,
                qk_ref, qkp_ref, qkn_ref, v_ref, g_ref, gt_ref, cw_ref, gb_ref, gbt_ref,
                o_ref, ct_ref, n_ref, m_ref):
    L = M_CHUNK
    c = pl.program_id(1)
    n_live = n_lat + n_ctx

    @pl.when(c == 0)
    def _():
        ct_ref[...] = jnp.zeros_like(ct_ref)
        n_ref[...] = jnp.zeros_like(n_ref)
        m_ref[...] = jnp.zeros_like(m_ref)

    @pl.when(c >= n_live)
    def _():
        o_ref[...] = jnp.zeros_like(o_ref)

    @pl.when(c < n_live)
    def _():
        blk = _seq_block(c, direction, n_lat, n_ctx)
        first = jnp.logical_or(blk == 0, blk == n_lat)
        last = jnp.logical_or(blk == n_lat - 1, blk == n_live - 1)
        p = qk_ref[...].astype(F32)
        rows = _iota((L, 1), 0)
        prev_row = jnp.where(first, 0.0, qkp_ref[HALO - 1:HALO, :].astype(F32))
        next_row = jnp.where(last, 0.0, qkn_ref[0:1, :].astype(F32))
        pm = jnp.where(rows == 0, prev_row, pltpu.roll(p, 1, 0))
        pp = jnp.where(rows == L - 1, next_row, pltpu.roll(p, L - 1, 0))
        qkc = cw_ref[0:1, :] * pm + cw_ref[1:2, :] * p + cw_ref[2:3, :] * pp
        qk = qkc * _sigmoid(qkc)
        nq = M_HEADS * M_DQK
        q_all = qk[:, :nq]
        k_all = qk[:, nq:] * (M_DQK ** -0.5)
        v_all = v_ref[...]

        g = g_ref[...] + gb_ref[...]
        gt = gt_ref[...] + gbt_ref[...]
        r_i = _iota((L, L), 0)
        c_i = _iota((L, L), 1)
        causal = (r_i >= c_i) if direction == 0 else (r_i <= c_i)
        tri = jnp.where(causal, 1.0, 0.0).astype(BF16)
        tri_t = jnp.where(causal, 0.0, 1.0).astype(BF16) + jnp.where(r_i == c_i, 1.0, 0.0).astype(BF16)
        lf = _log_sigmoid(g)
        lft = _log_sigmoid(gt)
        b_col = sum(_dot(tri, part) for part in _split3(lf))
        b_row = sum(_dot(part, tri_t) for part in _split3(lft))
        e_last = L - 1 if direction == 0 else 0

        hs = []
        for hd in range(M_HEADS):
            ci = 16 * direction + hd
            cf = ci + 8
            q_h = q_all[:, hd * M_DQK:(hd + 1) * M_DQK]
            q_b = q_h.astype(BF16)
            k_b = k_all[:, hd * M_DQK:(hd + 1) * M_DQK].astype(BF16)
            hs.append(dict(b_c=b_col[:, cf:cf + 1], i_c=g[:, ci:ci + 1], b_r=b_row[cf:cf + 1, :],
                           i_r=gt[ci:ci + 1, :], m_st=m_ref[hd][0:1, 0:1], n_st=n_ref[hd][0:1, :],
                           q_h=q_h, q_b=q_b, k_b=k_b, v_h=v_all[:, hd * M_DV:(hd + 1) * M_DV].astype(F32),
                           qk=_dot_nt(q_b, k_b), qc=_dot(q_b, ct_ref[hd].astype(BF16))))
        for s in hs:
            dmat = jnp.where(causal, s["b_c"] - s["b_r"] + s["i_r"], NEG)
            inter = s["b_c"] + s["m_st"]
            s["m_t"] = jnp.maximum(inter, jnp.max(dmat, axis=-1, keepdims=True))
            s["s"] = s.pop("qk") * jnp.exp(dmat - s["m_t"])
            s["w_inter"] = jnp.exp(inter - s["m_t"])
        for s in hs:
            s["sv"] = _dot(s["s"].astype(BF16), s["v_h"].astype(BF16))
        for hd, s in enumerate(hs):
            num = s.pop("sv") + s["w_inter"] * s.pop("qc")
            den = (jnp.sum(s.pop("s"), axis=-1, keepdims=True)
                   + s.pop("w_inter") * jnp.sum(s.pop("q_h") * s["n_st"], axis=-1, keepdims=True))
            o_ref[:, hd * M_DV:(hd + 1) * M_DV] = num / jnp.maximum(jnp.abs(den), jnp.exp(-s.pop("m_t")))
        for s in hs:
            b_last = s["b_c"][e_last:e_last + 1, :]
            g_c = b_last - s["b_c"] + s["i_c"]
            g_r = b_last - s["b_r"] + s["i_r"]
            s["m_new"] = jnp.maximum(b_last + s["m_st"], jnp.max(g_r, axis=-1, keepdims=True))
            s["decay"] = jnp.exp(b_last + s["m_st"] - s["m_new"])
            s["kv"] = _dot_tn(s["k_b"], (jnp.exp(g_c - s["m_new"]) * s["v_h"]).astype(BF16))
            s["kn"] = _dot(jnp.broadcast_to(jnp.exp(g_r - s["m_new"]), (8, L)).astype(BF16), s["k_b"])[0:1, :]
        for hd, s in enumerate(hs):
            ct_ref[hd] = s["decay"] * ct_ref[hd] + s["kv"]
            n_ref[hd] = jnp.broadcast_to(s["decay"] * s["n_st"] + s["kn"], (8, M_DQK))
            m_ref[hd] = jnp.broadcast_to(s["m_new"], (8, 128))


def _mlstm_scan(p, gates, gates_t, conv_w, gb_row, gb_col, direction, s_len, c_len):
    b, tt, _ = p.shape
    L = M_CHUNK
    n_lat, n_ctx = s_len // L, c_len // L
    n_all = tt // L
    nqk = 2 * M_HEADS * M_DQK
    nv = M_HEADS * M_DV

    def blk(c):
        return jnp.where(c < n_lat + n_ctx, _seq_block(c, direction, n_lat, n_ctx), c)

    rh = L // HALO
    return pl.pallas_call(
        functools.partial(_mlstm_body, direction, n_lat, n_ctx),
        grid=(b, n_all),
        in_specs=[pl.BlockSpec((None, L, nqk), lambda bi, c: (bi, blk(c), 0)),
                  pl.BlockSpec((None, HALO, nqk), lambda bi, c: (bi, jnp.maximum(blk(c) * rh - 1, 0), 0)),
                  pl.BlockSpec((None, HALO, nqk),
                               lambda bi, c: (bi, jnp.minimum((blk(c) + 1) * rh, tt // HALO - 1), 0)),
                  pl.BlockSpec((None, L, nv), lambda bi, c: (bi, blk(c), 1)),
                  pl.BlockSpec((None, L, 128), lambda bi, c: (bi, blk(c), 0)),
                  pl.BlockSpec((None, 32, L), lambda bi, c: (bi, 0, blk(c))),
                  pl.BlockSpec((3, nqk), lambda bi, c: (0, 0)),
                  pl.BlockSpec((1, 128), lambda bi, c: (0, 0)),
                  pl.BlockSpec((32, 1), lambda bi, c: (0, 0))],
        out_specs=pl.BlockSpec((None, L, nv), lambda bi, c: (bi, blk(c), 0)),
        out_shape=jax.ShapeDtypeStruct((b, tt, nv), F32),
        scratch_shapes=[pltpu.VMEM((M_HEADS, M_DQK, M_DV), F32),
                        pltpu.VMEM((M_HEADS, 8, M_DQK), F32),
                        pltpu.VMEM((M_HEADS, 8, 128), F32)],
        compiler_params=_cp("parallel", "arbitrary"),
        name="mlstm_scan_%d" % direction,
    )(p, p, p, p, gates, gates_t, conv_w, gb_row, gb_col)


def _out_epilogue(z_b, w_ref, res, gate_ref, lg_ref, lb_ref, o_ref):
    y = _dot(z_b, w_ref[...])
    o_ref[...] = _layer_norm(ALPHA * res + gate_ref[...] * y, lg_ref[...], lb_ref[...])


def _mlstm_out_body(s_len, hf_ref, hb_ref, og_ref, nw_ref, w_ref, resx_ref, resc_ref, gate_ref, lg_ref, lb_ref,
                    o_ref, z_ref):
    ones = jnp.full((M_DV, M_DV), 1.0, BF16)
    for hd in range(M_HEADS):
        sl = slice(hd * M_DV, (hd + 1) * M_DV)
        hh = hf_ref[:, sl] + hb_ref[:, sl]
        mu = _dot(hh.astype(BF16), ones) * (1.0 / M_DV)
        hc = hh - mu
        var = _dot((hc * hc).astype(BF16), ones) * (1.0 / M_DV)
        hn = hc * lax.rsqrt(var + M_NORM_EPS) * nw_ref[:, sl]
        z_ref[:, sl] = (_sigmoid(og_ref[:, sl].astype(F32)) * hn).astype(BF16)
    res = jnp.where(pl.program_id(1) * TMO >= s_len, resc_ref[...], resx_ref[...])
    _out_epilogue(z_ref[...], w_ref, res, gate_ref, lg_ref, lb_ref, o_ref)


def _rwkv_out_body(yf_ref, yb_ref, y0_ref, bon_ref, gt_ref, lw_ref, lbx_ref,
                   w_ref, res_ref, gate_ref, lg_ref, lb_ref, o_ref, z_ref):
    ones = _group_ones(R_GROUP, R_HEAD)
    d = o_ref.shape[-1]
    for gi in range(d // R_GROUP):
        sl = slice(gi * R_GROUP, (gi + 1) * R_GROUP)
        y = yf_ref[:, sl].astype(F32) + yb_ref[:, sl].astype(F32) + y0_ref[:, sl]
        mu = _dot(y.astype(BF16), ones) * (1.0 / R_HEAD)
        yc = y - mu
        var = _dot((yc * yc).astype(BF16), ones) * (1.0 / R_HEAD)
        yn = yc * lax.rsqrt(var + R_GN_EPS) * lw_ref[:, sl] + lbx_ref[:, sl]
        z_ref[:, sl] = ((yn + bon_ref[:, sl].astype(F32)) * gt_ref[:, sl].astype(F32)).astype(BF16)
    _out_epilogue(z_ref[...], w_ref, res_ref[...], gate_ref, lg_ref, lb_ref, o_ref)


def _out_call(body, streams, stream_col, vecs, w, res, mod, ln_g, ln_b, batch, s_len, n_tiles, name):
    res = res if isinstance(res, tuple) else (res,)
    b, _, d = res[0].shape
    row = lambda bi, i: jnp.where(i * TMO >= s_len, batch, bi)
    tile = lambda col: pl.BlockSpec((None, TMO, d), lambda bi, i: (bi, i, col))
    vec = pl.BlockSpec((1, d), lambda bi, i: (0, 0))
    n_lat, n_ctx = s_len // TMO, TM // TMO
    res_specs = [tile(0)] if len(res) == 1 else [
        pl.BlockSpec((None, TMO, d), lambda bi, i: (bi, jnp.minimum(i, n_lat - 1), 0)),
        pl.BlockSpec((None, TMO, d), lambda bi, i: (bi, jnp.clip(i - n_lat, 0, n_ctx - 1), 0))]
    return pl.pallas_call(
        body,
        grid=(b, n_tiles),
        in_specs=[tile(col) for col in stream_col] + [vec] * len(vecs)
                 + [pl.BlockSpec((d, d), lambda bi, i: (0, 0))] + res_specs + [_mod_spec(d, 2, row), vec, vec],
        out_specs=tile(0),
        out_shape=jax.ShapeDtypeStruct((b, n_tiles * TMO, d), F32),
        scratch_shapes=[pltpu.VMEM((TMO, d), BF16)],
        compiler_params=_cp("parallel", "parallel"),
        name=name,
    )(*streams, *vecs, w, *res, mod, ln_g, ln_b)


def _mlp_body(h_ref, sc_ref, sh_ref, gate_ref, w1_ref, w2_ref, lg_ref, lb_ref, o_ref, u_ref, acc_ref):
    k = pl.program_id(2)

    @pl.when(k == 0)
    def _():
        u_ref[...] = (h_ref[...] * (1.0 + sc_ref[...]) + sh_ref[...]).astype(BF16)
        acc_ref[...] = jnp.zeros_like(acc_ref)

    a = jnp.maximum(_dot(u_ref[...], w1_ref[...]), 0.0)
    acc_ref[...] += _dot((a * a).astype(BF16), w2_ref[...])

    @pl.when(k == pl.num_programs(2) - 1)
    def _():
        o_ref[...] = _layer_norm(ALPHA * h_ref[...] + gate_ref[...] * acc_ref[...], lg_ref[...], lb_ref[...])


def _mlp(h, mod, w1, w2, ln_g, ln_b, batch, ctx_tile, n_tiles, name):
    b, _, d = h.shape
    dff = w1.shape[1]
    row = lambda bi, i, k: jnp.where(i == ctx_tile, batch, bi)
    tile = pl.BlockSpec((None, TM, d), lambda bi, i, k: (bi, i, 0))
    vec = pl.BlockSpec((1, d), lambda bi, i, k: (0, 0))
    return pl.pallas_call(
        _mlp_body,
        grid=(b, n_tiles, dff // FF_TILE),
        in_specs=[tile, _mod_spec(d, 4, row), _mod_spec(d, 3, row), _mod_spec(d, 5, row),
                  pl.BlockSpec((d, FF_TILE), lambda bi, i, k: (0, k)),
                  pl.BlockSpec((FF_TILE, d), lambda bi, i, k: (k, 0)), vec, vec],
        out_specs=tile,
        out_shape=jax.ShapeDtypeStruct((b, n_tiles * TM, d), F32),
        scratch_shapes=[pltpu.VMEM((TM, d), BF16), pltpu.VMEM((TM, d), F32)],
        compiler_params=_cp("parallel", "parallel", "arbitrary"),
        name=name,
    )(h, mod, mod, mod, w1, w2, ln_g, ln_b)


def _rproj_body(ctx_tile, c_len, n_lat_tiles,
                h_ref, hp_ref, hn_ref, sc_ref, sh_ref, mu_ref, w_ref, wl_ref, o_ref, ol_ref, u_ref, xx_ref):
    i = pl.program_id(1)

    @pl.when(pl.program_id(2) == 0)
    def _():
        is_ctx = i == ctx_tile
        sc = 1.0 + sc_ref[...]
        sh = sh_ref[...]
        u = h_ref[...] * sc + sh
        u_prev = hp_ref[...] * sc + sh
        u_next = hn_ref[...] * sc + sh
        u_ref[...] = u
        t = _iota((TM, 1), 0)
        tw = jnp.bitwise_and(t, GRID_W - 1)
        q = u.shape[1] // 4
        flag = lambda cond: jnp.where(cond, 1, 0)
        ctx_prev, ctx_next = flag(t != 0), flag(t != c_len - 1)
        lat_left, lat_right = flag(tw != 0), flag(tw != GRID_W - 1)
        lat_up = jnp.where(i == 0, flag(t >= GRID_W), 1)
        lat_down = jnp.where(i == n_lat_tiles - 1, flag(t < TM - GRID_W), 1)
        for qi in range(4):
            sl = slice(qi * q, (qi + 1) * q)
            uq = u[:, sl]
            before = pltpu.roll(uq, 1, 0)
            after = pltpu.roll(uq, TM - 1, 0)
            if qi == 0:
                src = before
                keep = jnp.where(is_ctx, ctx_prev, lat_left)
            elif qi == 1:
                src = jnp.where(is_ctx, before, after)
                keep = jnp.where(is_ctx, ctx_prev, lat_right)
            elif qi == 2:
                above = jnp.concatenate([u_prev[:, sl], uq[:TM - GRID_W]], axis=0)
                src = jnp.where(is_ctx, after, above)
                keep = jnp.where(is_ctx, ctx_next, lat_up)
            else:
                below = jnp.concatenate([uq[GRID_W:], u_next[:, sl]], axis=0)
                src = jnp.where(is_ctx, after, below)
                keep = jnp.where(is_ctx, ctx_next, lat_down)
            xx_ref[:, sl] = jnp.where(keep != 0, src, 0.0) - uq

        for ti in range(ol_ref.shape[-1] // R_LORA_PAD):
            sl = slice(ti * R_LORA_PAD, (ti + 1) * R_LORA_PAD)
            x = (u_ref[...] + xx_ref[...] * mu_ref[3 + max(ti - 1, 0)]).astype(BF16)
            ol_ref[:, sl] = _dot(x, wl_ref[:, sl])

    x = (u_ref[...] + xx_ref[...] * mu_ref[pl.program_id(2)]).astype(BF16)
    o_ref[...] = _dot(x, w_ref[...]).astype(o_ref.dtype)


def _rproj(h, mod, mu, w, w_lora, batch, s_len, c_len):
    b, tt, d = h.shape
    n = w.shape[1]
    nl = w_lora.shape[1]
    nj = n // d
    ctx_tile = s_len // TM
    row = lambda bi, i, j: jnp.where(i == ctx_tile, batch, bi)
    r64 = TM // GRID_W
    return pl.pallas_call(
        functools.partial(_rproj_body, ctx_tile, c_len, s_len // TM),
        grid=(b, tt // TM, nj),
        in_specs=[pl.BlockSpec((None, TM, d), lambda bi, i, j: (bi, i, 0)),
                  pl.BlockSpec((None, GRID_W, d), lambda bi, i, j: (bi, jnp.maximum(i * r64 - 1, 0), 0)),
                  pl.BlockSpec((None, GRID_W, d),
                               lambda bi, i, j: (bi, jnp.minimum((i + 1) * r64, tt // GRID_W - 1), 0)),
                  _mod_spec(d, 1, row), _mod_spec(d, 0, row),
                  pl.BlockSpec((8, 1, d), lambda bi, i, j: (0, 0, 0)),
                  pl.BlockSpec((d, d), lambda bi, i, j: (0, j)),
                  pl.BlockSpec((d, nl), lambda bi, i, j: (0, 0))],
        out_specs=[pl.BlockSpec((None, TM, d), lambda bi, i, j: (bi, i, j)),
                   pl.BlockSpec((None, TM, nl), lambda bi, i, j: (bi, i, 0))],
        out_shape=[jax.ShapeDtypeStruct((b, tt, n), BF16), jax.ShapeDtypeStruct((b, tt, nl), F32)],
        scratch_shapes=[pltpu.VMEM((TM, d), F32), pltpu.VMEM((TM, d), F32)],
        compiler_params=_cp("parallel", "parallel", "arbitrary"),
        name="rwkv_proj",
    )(h, h, h, mod, mod, mu, w, w_lora)


def _block_diag(x, mask):
    return jnp.where(mask, jnp.concatenate([x] * (R_GROUP // R_CHUNK), axis=0), 0.0)


def _rloc_body(r_ref, k_ref, v_ref, lo_ref, g2_ref, w2_ref, a2_ref, vec_ref,
               rf_ref, rb_ref, y0_ref, bon_ref, gate_ref, gf_ref, gb_ref, hf_ref, hb_ref, dgf_ref, dgb_ref,
               rs_ref, vs_ref, kk_ref, lw_ref, kd_ref, bb_ref):
    L = R_CHUNK
    W = R_GROUP
    lo = lo_ref[...]
    r = r_ref[...].astype(F32)
    k = k_ref[...].astype(F32)
    v = v_ref[...].astype(F32)
    w0 = (vec_ref[0:1, :], vec_ref[1:2, :])
    a0 = (vec_ref[2:3, :], vec_ref[3:4, :])
    k_k = vec_ref[4:5, :]
    k_a = vec_ref[5:6, :]
    r_k = vec_ref[6:7, :]
    ones = _group_ones(W, R_HEAD)

    gate_ref[...] = _dot(_sigmoid(lo[:, :R_GATE_LORA]).astype(BF16), g2_ref[...]).astype(gate_ref.dtype)
    kraw = k * k_k
    ss = _dot((kraw * kraw).astype(BF16), ones)
    kk = kraw / jnp.maximum(jnp.sqrt(ss), 1e-12)
    rs_ref[...] = r
    vs_ref[...] = v
    kk_ref[...] = kk
    kd_sum = jnp.zeros_like(k)
    for d in range(2):
        wl = lo[:, R_GATE_LORA + d * R_LORA_PAD:R_GATE_LORA + (d + 1) * R_LORA_PAD]
        al = lo[:, R_GATE_LORA + (2 + d) * R_LORA_PAD:R_GATE_LORA + (3 + d) * R_LORA_PAD]
        lwx = w0[d] + _dot(jnp.tanh(wl).astype(BF16), w2_ref[d])
        lw_ref[d] = -_sigmoid(lwx) * (jnp.e ** -0.5)
        iclr = _sigmoid(a0[d] + _dot(al.astype(BF16), a2_ref[d]))
        kd = k * (1.0 + (iclr - 1.0) * k_a)
        kd_ref[d] = kd
        bb_ref[d] = kk * iclr
        kd_sum = kd_sum + kd
    bon_ref[...] = (_dot((r * r_k * kd_sum).astype(BF16), ones) * v).astype(bon_ref.dtype)

    lane_head = _iota((W, W), 1) // R_HEAD
    bd_mask = _iota((W, W), 0) // L == lane_head
    t_i = _iota((L, W), 0)
    s_i = jnp.bitwise_and(_iota((L, W), 1), L - 1)
    eye = jnp.where(t_i == s_i, 1.0, 0.0)
    ts_xor = jnp.bitwise_xor(t_i, s_i)
    diag2 = jnp.right_shift(ts_xor, 1) == 0
    off_masks = [jnp.right_shift(ts_xor, lv) == 1 for lv in range(1, 6)]
    r_i = _iota((L, L), 0)
    c_i = _iota((L, L), 1)
    outs = ((rf_ref, gf_ref, hf_ref), (rb_ref, gb_ref, hb_ref))

    tris = (jnp.where(r_i >= c_i, 1.0, 0.0).astype(BF16), jnp.where(r_i <= c_i, 1.0, 0.0).astype(BF16))
    stricts = (t_i > s_i, t_i < s_i)
    incls = (t_i >= s_i, t_i <= s_i)
    bd = lambda x: _block_diag(x, bd_mask).astype(BF16)

    def head_t(x):
        xt = x.T
        return jnp.concatenate([xt[hd * R_HEAD:(hd + 1) * R_HEAD, :] for hd in range(W // R_HEAD)], axis=1)

    n_chunks = R_BLOCK // L
    inst = [(ci, d) for ci in range(n_chunks) for d in range(2)]
    rows_of = [pl.ds(ci * L, L) for ci in range(n_chunks)]
    vcs = [vs_ref[rows_of[ci], :] for ci in range(n_chunks)]
    bd_vs = [bd(vc) for vc in vcs]

    st = []
    for ci, d in inst:
        rows = rows_of[ci]
        lwc = lw_ref[d, rows, :]
        cum = sum(_dot(tris[d], part) for part in _split3(lwc))
        e_last = L - 1 if d == 0 else 0
        tot = cum[e_last:e_last + 1, :]
        p_inv = jnp.exp(-cum)
        p_end = jnp.exp(tot - cum)
        bc = bb_ref[d, rows, :]
        kdc = kd_ref[d, rows, :]
        st.append(dict(tot=tot, a_t=-kk_ref[rows, :] * jnp.exp(cum - lwc), r_t=rs_ref[rows, :] * jnp.exp(cum),
                       b_inv=bc * p_inv, k_inv=kdc * p_inv,
                       b_end=head_t(bc * p_end).astype(BF16), k_end=head_t(kdc * p_end).astype(BF16)))
    for s, (ci, d) in zip(st, inst):
        z = _dot_nt(jnp.concatenate([s["a_t"], s["r_t"]], axis=0).astype(BF16),
                    jnp.concatenate([bd(s.pop("b_inv")), bd(s.pop("k_inv"))], axis=0))
        s["a_ab"] = jnp.where(stricts[d], z[:L, :W], 0.0)
        s["a_ak"] = jnp.where(stricts[d], z[:L, W:], 0.0).astype(BF16)
        s["m_rb"] = jnp.where(incls[d], z[L:, :W], 0.0).astype(BF16)
        s["m_rk"] = jnp.where(incls[d], z[L:, W:], 0.0).astype(BF16)
        s["tinv"] = eye + jnp.where(diag2, s["a_ab"], 0.0)
    for ci in range(n_chunks):
        pair = [s for s, (cj, d) in zip(st, inst) if cj == ci]
        lhs = [s.pop("a_ak") for s in pair] + [s.pop("m_rk") for s in pair] + [s.pop("k_end") for s in pair]
        xv = _dot(jnp.concatenate(lhs, axis=0), bd_vs[ci])
        for d, s in enumerate(pair):
            s["akv"] = xv[d * L:(d + 1) * L]
            s["mkv"] = xv[(2 + d) * L:(3 + d) * L]
            s["zk"] = xv[(4 + d) * L:(5 + d) * L]
    for off in off_masks:
        for s in st:
            s["x1"] = _dot(s["tinv"].astype(BF16), bd(jnp.where(off, s["a_ab"], 0.0)))
        for s in st:
            s["tinv"] = s["tinv"] + _dot(s.pop("x1").astype(BF16), bd(s["tinv"]))
    for s in st:
        x = _dot(s.pop("tinv").astype(BF16), jnp.concatenate([bd(s.pop("a_t")), bd(s.pop("akv"))], axis=1))
        s["a_p"] = x[:, :W]
        s["u0"] = x[:, W:]
    y0s = [jnp.zeros((L, W), F32) for _ in range(n_chunks)]
    for s, (ci, d) in zip(st, inst):
        yz = _dot(jnp.concatenate([s.pop("m_rb"), s.pop("b_end")], axis=0),
                  jnp.concatenate([bd(s.pop("a_p")), bd(s.pop("u0"))], axis=1))
        outs[d][0][rows_of[ci], :] = (s.pop("r_t") + yz[:L, :W]).astype(BF16)
        y0s[ci] = y0s[ci] + yz[:L, W:] + s.pop("mkv")
        outs[d][1][rows_of[ci], :] = yz[L:, :W].astype(BF16)
        outs[d][2][rows_of[ci], :] = (yz[L:, W:] + s.pop("zk")).astype(BF16)
    for ci in range(n_chunks):
        y0_ref[rows_of[ci], :] = y0s[ci]
    for dg_ref in (dgf_ref, dgb_ref):
        dg_ref[n_chunks:, :] = jnp.zeros((8 - n_chunks, W), F32)
    for s, (ci, d) in zip(st, inst):
        (dgf_ref, dgb_ref)[d][ci:ci + 1, :] = jnp.exp(s["tot"])


def _rloc(rkv, lora, g2, w2p, a2p, vecs, s_len, c_len):
    b, tt, n3 = rkv.shape
    d = n3 // 3
    nb = (s_len + c_len) // R_BLOCK
    ng = d // R_GROUP
    tok = lambda col: pl.BlockSpec((None, R_BLOCK, R_GROUP), lambda bi, i, j: (bi, i, col * ng + j))
    out_tok = pl.BlockSpec((None, R_BLOCK, R_GROUP), lambda bi, i, j: (bi, i, j))
    f32_out = jax.ShapeDtypeStruct((b, tt, d), F32)
    bf_out = jax.ShapeDtypeStruct((b, tt, d), BF16)
    sc = lambda *shape: pltpu.VMEM(shape, F32)
    return pl.pallas_call(
        _rloc_body,
        grid=(b, nb, ng),
        in_specs=[tok(0), tok(1), tok(2),
                  pl.BlockSpec((None, R_BLOCK, lora.shape[2]), lambda bi, i, j: (bi, i, 0)),
                  pl.BlockSpec((R_GATE_LORA, R_GROUP), lambda bi, i, j: (0, j)),
                  pl.BlockSpec((2, R_LORA_PAD, R_GROUP), lambda bi, i, j: (0, 0, j)),
                  pl.BlockSpec((2, R_LORA_PAD, R_GROUP), lambda bi, i, j: (0, 0, j)),
                  pl.BlockSpec((8, R_GROUP), lambda bi, i, j: (0, j))],
        out_specs=[out_tok] * 9 + [pl.BlockSpec((None, None, 8, R_GROUP), lambda bi, i, j: (bi, i, 0, j))] * 2,
        out_shape=[bf_out, bf_out, f32_out, bf_out, bf_out, bf_out, bf_out, bf_out, bf_out,
                   jax.ShapeDtypeStruct((b, nb, 8, d), F32), jax.ShapeDtypeStruct((b, nb, 8, d), F32)],
        scratch_shapes=[sc(R_BLOCK, R_GROUP), sc(R_BLOCK, R_GROUP), sc(R_BLOCK, R_GROUP),
                        sc(2, R_BLOCK, R_GROUP), sc(2, R_BLOCK, R_GROUP), sc(2, R_BLOCK, R_GROUP)],
        compiler_params=_cp("parallel", "parallel", "arbitrary"),
        name="rwkv_local",
    )(rkv, rkv, rkv, lora, g2, w2p, a2p, vecs)


def _rseq_body(n_lat, n_ctx, gf_ref, gb_ref, hf_ref, hb_ref, rf_ref, rb_ref, dgf_ref, dgb_ref,
               yf_ref, yb_ref, st_ref):
    L = R_CHUNK
    W = R_GROUP
    c = pl.program_id(1)

    @pl.when(c == 0)
    def _():
        st_ref[...] = jnp.zeros_like(st_ref)

    bd_mask = _iota((W, W), 0) // L == _iota((W, W), 1) // R_HEAD
    eye = jnp.where(_iota((L, W), 0) == jnp.bitwise_and(_iota((L, W), 1), L - 1), 1.0, 0.0)
    ins = ((gf_ref, hf_ref, rf_ref, dgf_ref, yf_ref), (gb_ref, hb_ref, rb_ref, dgb_ref, yb_ref))
    for d in range(2):
        g_ref, ha_ref, rp_ref, dg_ref, y_ref = ins[d]
        chunk_in_block = jnp.bitwise_and(_seq_block(c, d, n_lat, n_ctx), R_BLOCK // L - 1)
        for gi in range(st_ref.shape[-1] // W):
            sl = slice(gi * W, (gi + 1) * W)
            st = st_ref[d, :, sl]
            st_hi = st.astype(BF16)
            st_lo = (st - st_hi.astype(F32)).astype(BF16)
            g_hi, g_lo = _split2(g_ref[:, sl].astype(F32) + eye * dg_ref[pl.ds(chunk_in_block, 1), sl])
            rp = rp_ref[:, sl]
            bd_hi = _block_diag(st_hi.astype(F32), bd_mask).astype(BF16)
            bd_lo = _block_diag(st_lo.astype(F32), bd_mask).astype(BF16)
            o1 = _dot(jnp.concatenate([g_hi, g_lo, rp], axis=0), bd_hi)
            o2 = _dot(jnp.concatenate([g_hi, rp], axis=0), bd_lo)
            st_ref[d, :, sl] = o1[:L] + o1[L:2 * L] + o2[:L] + ha_ref[:, sl].astype(F32)
            y_ref[:, sl] = (o1[2 * L:] + o2[L:]).astype(y_ref.dtype)


def _rseq(gf, gb, hf, hb, rf, rb, dgf, dgb, s_len, c_len):
    b, tt, d = gf.shape
    L = R_CHUNK
    n_lat, n_ctx = s_len // L, c_len // L
    per_block = R_BLOCK // L
    spec = lambda direction: pl.BlockSpec(
        (None, L, d), lambda bi, c: (bi, _seq_block(c, direction, n_lat, n_ctx), 0))
    dspec = lambda direction: pl.BlockSpec(
        (None, None, 8, d), lambda bi, c: (bi, _seq_block(c, direction, n_lat, n_ctx) // per_block, 0, 0))
    out = jax.ShapeDtypeStruct((b, tt, d), BF16)
    return pl.pallas_call(
        functools.partial(_rseq_body, n_lat, n_ctx),
        grid=(b, n_lat + n_ctx),
        in_specs=[spec(0), spec(1), spec(0), spec(1), spec(0), spec(1), dspec(0), dspec(1)],
        out_specs=[spec(0), spec(1)],
        out_shape=[out, out],
        scratch_shapes=[pltpu.VMEM((2, L, d), F32)],
        compiler_params=_cp("parallel", "arbitrary"),
        name="rwkv_seq",
    )(gf, gb, hf, hb, rf, rb, dgf, dgb)


def kernel(x, c, ctx, c_ctx, ada_w, ada_b, ln_g, ln_b, mlp_w1, mlp_w2, m_w_in, m_conv, m_gate_b, m_norm_w, m_w_out, r_w_in, r_mu, r_g2, r_w0, r_w2, r_a0, r_a2, r_k_k, r_k_a, r_r_k, r_lnx_w, r_lnx_b, r_w_out):
    batch, s_len, d = x.shape
    c_len = ctx.shape[1]
    assert batch < MOD_ROWS and s_len % TM == 0 and c_len % M_CHUNK == 0 and c_len <= TM
    assert s_len % GRID_W == 0 and d % R_GROUP == 0 and ada_w.shape[0] == DEPTH == 2
    tt = s_len + TM
    ctx_tile = s_len // TM
    row = lambda a: a.reshape(1, -1)

    ctx_pad = jnp.concatenate([ctx, jnp.zeros((batch, TM - c_len, d), F32)], axis=1)
    cvec = jnp.concatenate([c, c_ctx[None], jnp.zeros((MOD_ROWS - batch - 1, d), F32)], axis=0)
    mod = _ada(cvec, ada_w, ada_b).reshape(DEPTH, MOD_ROWS, 1, 6 * d)

    nqk = 2 * M_HEADS * M_DQK
    nv = M_HEADS * M_DV
    w_in = m_w_in[0]
    w_gates = jnp.pad(w_in[:, nqk + 2 * nv:], ((0, 0), (0, 128 - 4 * M_HEADS))).astype(BF16)
    p, gates = _proj(x, ctx_pad, mod[0], w_in[:, :nqk + 2 * nv].astype(BF16), w_gates, d, batch, "mlstm_proj")
    gates_t = jnp.swapaxes(gates[:, :, :4 * M_HEADS], 1, 2)
    gb = m_gate_b[0].reshape(-1)
    gb_row = jnp.pad(gb, (0, 128 - gb.shape[0])).reshape(1, 128)
    gb_col = gb.reshape(-1, 1)
    scan = functools.partial(_mlstm_scan, p, gates, gates_t, m_conv[0], gb_row, gb_col)
    h_f = scan(0, s_len, c_len)
    h_b = scan(1, s_len, c_len)
    h = _out_call(functools.partial(_mlstm_out_body, s_len), (h_f, h_b, p), (0, 0, (nqk + nv) // d),
                  (row(m_norm_w[0]),), m_w_out[0].astype(BF16), (x, ctx_pad), mod[0], row(ln_g[0, 0]), row(ln_b[0, 0]),
                  batch, s_len, tt // TMO, "mlstm_out")
    h = _mlp(h, mod[0], mlp_w1[0].astype(BF16), mlp_w2[0].astype(BF16), row(ln_g[0, 1]), row(ln_b[0, 1]),
             batch, ctx_tile, tt // TM, "mlp0")

    w_in = r_w_in[0]
    mu = r_mu[0].reshape(8, 1, d)
    lo_cols = [w_in[:, 3 * d:3 * d + R_GATE_LORA]]
    for gi in range(4):
        start = 3 * d + R_GATE_LORA + gi * R_LORA
        lo_cols.append(jnp.pad(w_in[:, start:start + R_LORA], ((0, 0), (0, R_LORA_PAD - R_LORA))))
    w_lora = jnp.concatenate(lo_cols, axis=1).astype(BF16)
    rkv, lora = _rproj(h, mod[1], mu, w_in[:, :3 * d].astype(BF16), w_lora, batch, s_len, c_len)
    pad_rows = ((0, 0), (0, R_LORA_PAD - R_LORA), (0, 0))
    vecs = jnp.concatenate([r_w0[0], r_a0[0], row(r_k_k[0]), row(r_k_a[0]), row(r_r_k[0]),
                            jnp.zeros((1, d), F32)], axis=0)
    rf, rb, y0, bonus, gate, g_f, g_b, ha_f, ha_b, dg_f, dg_b = _rloc(
        rkv, lora, r_g2[0].astype(BF16), jnp.pad(r_w2[0], pad_rows).astype(BF16),
        jnp.pad(r_a2[0], pad_rows).astype(BF16), vecs, s_len, c_len)
    y_f, y_b = _rseq(g_f, g_b, ha_f, ha_b, rf, rb, dg_f, dg_b, s_len, c_len)
    h1 = _out_call(_rwkv_out_body, (y_f, y_b, y0, bonus, gate), (0, 0, 0, 0, 0),
                   (row(r_lnx_w[0]), row(r_lnx_b[0])), r_w_out[0].astype(BF16), h, mod[1],
                   row(ln_g[1, 0]), row(ln_b[1, 0]), batch, s_len, s_len // TMO, "rwkv_out")
    return _mlp(h1, mod[1], mlp_w1[1].astype(BF16), mlp_w2[1].astype(BF16), row(ln_g[1, 1]), row(ln_b[1, 1]),
                batch, ctx_tile, s_len // TM, "mlp1")
```

```python
import functools

import jax
import jax.numpy as jnp
from jax import lax
from jax.experimental import pallas as pl
from jax.experimental.pallas import tpu as pltpu

F32 = jnp.float32
BF16 = jnp.bfloat16

DEPTH = 2
GRID_W = 64
ALPHA = (2 * DEPTH) ** 0.25
LN_EPS = 1e-5
M_HEADS = 8
M_DQK = 128
M_DV = 256
M_NORM_EPS = 1e-6
R_HEAD = 64
R_GN_EPS = 64e-5
R_LORA = 96
R_LORA_PAD = 128
R_GATE_LORA = 256

TM = 512
TMO = 256
M_CHUNK = 256
HALO = 16
R_CHUNK = 64
R_BLOCK = 256
R_GROUP = 256
FF_TILE = 1024
MOD_ROWS = 8
NEG = -1e30
VMEM_LIMIT = 52 * 1024 * 1024


def _cp(*sem):
    return pltpu.CompilerParams(dimension_semantics=sem, vmem_limit_bytes=VMEM_LIMIT)


def _dot(a, b):
    return jnp.dot(a, b, preferred_element_type=F32)


def _dot_nt(a, b):
    return lax.dot_general(a, b, (((1,), (1,)), ((), ())), preferred_element_type=F32)


def _dot_tn(a, b):
    return lax.dot_general(a, b, (((0,), (0,)), ((), ())), preferred_element_type=F32)


def _split2(x):
    hi = x.astype(BF16)
    lo = (x - hi.astype(F32)).astype(BF16)
    return hi, lo


def _split3(x):
    hi = x.astype(BF16)
    r = x - hi.astype(F32)
    mid = r.astype(BF16)
    lo = (r - mid.astype(F32)).astype(BF16)
    return hi, mid, lo


def _sigmoid(x):
    return 1.0 / (1.0 + jnp.exp(-x))


def _log_sigmoid(x):
    return jnp.minimum(x, 0.0) - jnp.log(1.0 + jnp.exp(-jnp.abs(x)))


def _layer_norm(z, g, b):
    mu = jnp.mean(z, axis=-1, keepdims=True)
    zc = z - mu
    var = jnp.mean(zc * zc, axis=-1, keepdims=True)
    return zc * lax.rsqrt(var + LN_EPS) * g + b


def _iota(shape, dim):
    return lax.broadcasted_iota(jnp.int32, shape, dim)


def _group_ones(n, group):
    return jnp.where(_iota((n, n), 0) // group == _iota((n, n), 1) // group, 1.0, 0.0).astype(BF16)


def _ada_body(c_ref, w_ref, b_ref, o_ref):
    x = c_ref[...]
    x = x * _sigmoid(x)
    xh, xl = _split2(x)
    wh, wl = _split2(w_ref[...])
    o_ref[...] = _dot(xh, wh) + _dot(xh, wl) + _dot(xl, wh) + b_ref[...]


def _ada(cvec, ada_w, ada_b):
    depth, d, n = ada_w.shape
    tn = 1024
    return pl.pallas_call(
        _ada_body,
        grid=(depth, n // tn),
        in_specs=[pl.BlockSpec((MOD_ROWS, d), lambda l, j: (0, 0)),
                  pl.BlockSpec((None, d, tn), lambda l, j: (l, 0, j)),
                  pl.BlockSpec((None, 1, tn), lambda l, j: (l, 0, j))],
        out_specs=pl.BlockSpec((None, MOD_ROWS, tn), lambda l, j: (l, 0, j)),
        out_shape=jax.ShapeDtypeStruct((depth, MOD_ROWS, n), F32),
        compiler_params=_cp("parallel", "parallel"),
        name="ada_mod",
    )(cvec, ada_w, ada_b.reshape(depth, 1, n))


def _mod_spec(d, which, row_fn):
    return pl.BlockSpec((None, 1, d), lambda *g: (row_fn(*g), 0, which))


def _proj_body(ctx_tile, x_ref, c_ref, sc_ref, sh_ref, w_ref, ws_ref, o_ref, os_ref, u_ref):
    @pl.when(pl.program_id(2) == 0)
    def _():
        h = jnp.where(pl.program_id(1) == ctx_tile, c_ref[...], x_ref[...])
        u_ref[...] = (h * (1.0 + sc_ref[...]) + sh_ref[...]).astype(BF16)
        os_ref[...] = _dot(u_ref[...], ws_ref[...])

    o_ref[...] = _dot(u_ref[...], w_ref[...]).astype(o_ref.dtype)


def _proj(x, ctx_pad, mod, w, w_small, tn, batch, name):
    b, s_len, d = x.shape
    ctx_tile = s_len // TM
    tt = s_len + TM
    n = w.shape[1]
    ns = w_small.shape[1]
    nj = n // tn
    row = lambda bi, i, j: jnp.where(i == ctx_tile, batch, bi)
    return pl.pallas_call(
        functools.partial(_proj_body, ctx_tile),
        grid=(b, tt // TM, nj),
        in_specs=[pl.BlockSpec((None, TM, d), lambda bi, i, j: (bi, jnp.minimum(i, ctx_tile - 1), 0)),
                  pl.BlockSpec((None, TM, d), lambda bi, i, j: (bi, 0, 0)),
                  _mod_spec(d, 1, row), _mod_spec(d, 0, row),
                  pl.BlockSpec((d, tn), lambda bi, i, j: (0, j)),
                  pl.BlockSpec((d, ns), lambda bi, i, j: (0, 0))],
        out_specs=[pl.BlockSpec((None, TM, tn), lambda bi, i, j: (bi, i, j)),
                   pl.BlockSpec((None, TM, ns), lambda bi, i, j: (bi, i, 0))],
        out_shape=[jax.ShapeDtypeStruct((b, tt, n), BF16), jax.ShapeDtypeStruct((b, tt, ns), F32)],
        scratch_shapes=[pltpu.VMEM((TM, d), BF16)],
        compiler_params=_cp("parallel", "parallel", "arbitrary"),
        name=name,
    )(x, ctx_pad, mod, mod, w, w_small)


def _seq_block(c, direction, n_lat, n_ctx):
    if direction == 0:
        return jnp.where(c < n_ctx, n_lat + c, c - n_ctx)
    return jnp.where(c < n_ctx, n_lat + n_ctx - 1 - c, n_lat - 1 - (c - n_ctx))


def _mlstm_body(direction, n_lat, n_ctx,
                qk_ref, qkp_ref, qkn_ref, v_ref, g_ref, gt_ref, cw_ref, gb_ref, gbt_ref,
                o_ref, ct_ref, n_ref, m_ref):
    L = M_CHUNK
    c = pl.program_id(1)
    n_live = n_lat + n_ctx

    @pl.when(c == 0)
    def _():
        ct_ref[...] = jnp.zeros_like(ct_ref)
        n_ref[...] = jnp.zeros_like(n_ref)
        m_ref[...] = jnp.zeros_like(m_ref)

    @pl.when(c >= n_live)
    def _():
        o_ref[...] = jnp.zeros_like(o_ref)

    @pl.when(c < n_live)
    def _():
        blk = _seq_block(c, direction, n_lat, n_ctx)
        first = jnp.logical_or(blk == 0, blk == n_lat)
        last = jnp.logical_or(blk == n_lat - 1, blk == n_live - 1)
        p = qk_ref[...].astype(F32)
        rows = _iota((L, 1), 0)
        prev_row = jnp.where(first, 0.0, qkp_ref[HALO - 1:HALO, :].astype(F32))
        next_row = jnp.where(last, 0.0, qkn_ref[0:1, :].astype(F32))
        pm = jnp.where(rows == 0, prev_row, pltpu.roll(p, 1, 0))
        pp = jnp.where(rows == L - 1, next_row, pltpu.roll(p, L - 1, 0))
        qkc = cw_ref[0:1, :] * pm + cw_ref[1:2, :] * p + cw_ref[2:3, :] * pp
        qk = qkc * _sigmoid(qkc)
        nq = M_HEADS * M_DQK
        q_all = qk[:, :nq]
        k_all = qk[:, nq:] * (M_DQK ** -0.5)
        v_all = v_ref[...]

        g = g_ref[...] + gb_ref[...]
        gt = gt_ref[...] + gbt_ref[...]
        r_i = _iota((L, L), 0)
        c_i = _iota((L, L), 1)
        causal = (r_i >= c_i) if direction == 0 else (r_i <= c_i)
        tri = jnp.where(causal, 1.0, 0.0).astype(BF16)
        tri_t = jnp.where(causal, 0.0, 1.0).astype(BF16) + jnp.where(r_i == c_i, 1.0, 0.0).astype(BF16)
        lf = _log_sigmoid(g)
        lft = _log_sigmoid(gt)
        b_col = sum(_dot(tri, part) for part in _split3(lf))
        b_row = sum(_dot(part, tri_t) for part in _split3(lft))
        e_last = L - 1 if direction == 0 else 0

        hs = []
        for hd in range(M_HEADS):
            ci = 16 * direction + hd
            cf = ci + 8
            q_h = q_all[:, hd * M_DQK:(hd + 1) * M_DQK]
            q_b = q_h.astype(BF16)
            k_b = k_all[:, hd * M_DQK:(hd + 1) * M_DQK].astype(BF16)
            hs.append(dict(b_c=b_col[:, cf:cf + 1], i_c=g[:, ci:ci + 1], b_r=b_row[cf:cf + 1, :],
                           i_r=gt[ci:ci + 1, :], m_st=m_ref[hd][0:1, 0:1], n_st=n_ref[hd][0:1, :],
                           q_h=q_h, q_b=q_b, k_b=k_b, v_h=v_all[:, hd * M_DV:(hd + 1) * M_DV].astype(F32),
                           qk=_dot_nt(q_b, k_b), qc=_dot(q_b, ct_ref[hd].astype(BF16))))
        for s in hs:
            dmat = jnp.where(causal, s["b_c"] - s["b_r"] + s["i_r"], NEG)
            inter = s["b_c"] + s["m_st"]
            s["m_t"] = jnp.maximum(inter, jnp.max(dmat, axis=-1, keepdims=True))
            s["s"] = s.pop("qk") * jnp.exp(dmat - s["m_t"])
            s["w_inter"] = jnp.exp(inter - s["m_t"])
        for s in hs:
            s["sv"] = _dot(s["s"].astype(BF16), s["v_h"].astype(BF16))
        for hd, s in enumerate(hs):
            num = s.pop("sv") + s["w_inter"] * s.pop("qc")
            den = (jnp.sum(s.pop("s"), axis=-1, keepdims=True)
                   + s.pop("w_inter") * jnp.sum(s.pop("q_h") * s["n_st"], axis=-1, keepdims=True))
            o_ref[:, hd * M_DV:(hd + 1) * M_DV] = num / jnp.maximum(jnp.abs(den), jnp.exp(-s.pop("m_t")))
        for s in hs:
            b_last = s["b_c"][e_last:e_last + 1, :]
            g_c = b_last - s["b_c"] + s["i_c"]
            g_r = b_last - s["b_r"] + s["i_r"]
            s["m_new"] = jnp.maximum(b_last + s["m_st"], jnp.max(g_r, axis=-1, keepdims=True))
            s["decay"] = jnp.exp(b_last + s["m_st"] - s["m_new"])
            s["kv"] = _dot_tn(s["k_b"], (jnp.exp(g_c - s["m_new"]) * s["v_h"]).astype(BF16))
            s["kn"] = _dot(jnp.broadcast_to(jnp.exp(g_r - s["m_new"]), (8, L)).astype(BF16), s["k_b"])[0:1, :]
        for hd, s in enumerate(hs):
            ct_ref[hd] = s["decay"] * ct_ref[hd] + s["kv"]
            n_ref[hd] = jnp.broadcast_to(s["decay"] * s["n_st"] + s["kn"], (8, M_DQK))
            m_ref[hd] = jnp.broadcast_to(s["m_new"], (8, 128))


def _mlstm_scan(p, gates, gates_t, conv_w, gb_row, gb_col, direction, s_len, c_len):
    b, tt, _ = p.shape
    L = M_CHUNK
    n_lat, n_ctx = s_len // L, c_len // L
    n_all = tt // L
    nqk = 2 * M_HEADS * M_DQK
    nv = M_HEADS * M_DV

    def blk(c):
        return jnp.where(c < n_lat + n_ctx, _seq_block(c, direction, n_lat, n_ctx), c)

    rh = L // HALO
    return pl.pallas_call(
        functools.partial(_mlstm_body, direction, n_lat, n_ctx),
        grid=(b, n_all),
        in_specs=[pl.BlockSpec((None, L, nqk), lambda bi, c: (bi, blk(c), 0)),
                  pl.BlockSpec((None, HALO, nqk), lambda bi, c: (bi, jnp.maximum(blk(c) * rh - 1, 0), 0)),
                  pl.BlockSpec((None, HALO, nqk),
                               lambda bi, c: (bi, jnp.minimum((blk(c) + 1) * rh, tt // HALO - 1), 0)),
                  pl.BlockSpec((None, L, nv), lambda bi, c: (bi, blk(c), 1)),
                  pl.BlockSpec((None, L, 128), lambda bi, c: (bi, blk(c), 0)),
                  pl.BlockSpec((None, 32, L), lambda bi, c: (bi, 0, blk(c))),
                  pl.BlockSpec((3, nqk), lambda bi, c: (0, 0)),
                  pl.BlockSpec((1, 128), lambda bi, c: (0, 0)),
                  pl.BlockSpec((32, 1), lambda bi, c: (0, 0))],
        out_specs=pl.BlockSpec((None, L, nv), lambda bi, c: (bi, blk(c), 0)),
        out_shape=jax.ShapeDtypeStruct((b, tt, nv), F32),
        scratch_shapes=[pltpu.VMEM((M_HEADS, M_DQK, M_DV), F32),
                        pltpu.VMEM((M_HEADS, 8, M_DQK), F32),
                        pltpu.VMEM((M_HEADS, 8, 128), F32)],
        compiler_params=_cp("parallel", "arbitrary"),
        name="mlstm_scan_%d" % direction,
    )(p, p, p, p, gates, gates_t, conv_w, gb_row, gb_col)


def _out_epilogue(z_b, w_ref, res, gate_ref, lg_ref, lb_ref, o_ref):
    y = _dot(z_b, w_ref[...])
    o_ref[...] = _layer_norm(ALPHA * res + gate_ref[...] * y, lg_ref[...], lb_ref[...])


def _mlstm_out_body(s_len, hf_ref, hb_ref, og_ref, nw_ref, w_ref, resx_ref, resc_ref, gate_ref, lg_ref, lb_ref,
                    o_ref, z_ref):
    ones = jnp.full((M_DV, M_DV), 1.0, BF16)
    for hd in range(M_HEADS):
        sl = slice(hd * M_DV, (hd + 1) * M_DV)
        hh = hf_ref[:, sl] + hb_ref[:, sl]
        mu = _dot(hh.astype(BF16), ones) * (1.0 / M_DV)
        hc = hh - mu
        var = _dot((hc * hc).astype(BF16), ones) * (1.0 / M_DV)
        hn = hc * lax.rsqrt(var + M_NORM_EPS) * nw_ref[:, sl]
        z_ref[:, sl] = (_sigmoid(og_ref[:, sl].astype(F32)) * hn).astype(BF16)
    res = jnp.where(pl.program_id(1) * TMO >= s_len, resc_ref[...], resx_ref[...])
    _out_epilogue(z_ref[...], w_ref, res, gate_ref, lg_ref, lb_ref, o_ref)


def _rwkv_out_body(yf_ref, yb_ref, y0_ref, bon_ref, gt_ref, lw_ref, lbx_ref,
                   w_ref, res_ref, gate_ref, lg_ref, lb_ref, o_ref, z_ref):
    ones = _group_ones(R_GROUP, R_HEAD)
    d = o_ref.shape[-1]
    for gi in range(d // R_GROUP):
        sl = slice(gi * R_GROUP, (gi + 1) * R_GROUP)
        y = yf_ref[:, sl].astype(F32) + yb_ref[:, sl].astype(F32) + y0_ref[:, sl]
        mu = _dot(y.astype(BF16), ones) * (1.0 / R_HEAD)
        yc = y - mu
        var = _dot((yc * yc).astype(BF16), ones) * (1.0 / R_HEAD)
        yn = yc * lax.rsqrt(var + R_GN_EPS) * lw_ref[:, sl] + lbx_ref[:, sl]
        z_ref[:, sl] = ((yn + bon_ref[:, sl].astype(F32)) * gt_ref[:, sl].astype(F32)).astype(BF16)
    _out_epilogue(z_ref[...], w_ref, res_ref[...], gate_ref, lg_ref, lb_ref, o_ref)


def _out_call(body, streams, stream_col, vecs, w, res, mod, ln_g, ln_b, batch, s_len, n_tiles, name):
    res = res if isinstance(res, tuple) else (res,)
    b, _, d = res[0].shape
    row = lambda bi, i: jnp.where(i * TMO >= s_len, batch, bi)
    tile = lambda col: pl.BlockSpec((None, TMO, d), lambda bi, i: (bi, i, col))
    vec = pl.BlockSpec((1, d), lambda bi, i: (0, 0))
    n_lat, n_ctx = s_len // TMO, TM // TMO
    res_specs = [tile(0)] if len(res) == 1 else [
        pl.BlockSpec((None, TMO, d), lambda bi, i: (bi, jnp.minimum(i, n_lat - 1), 0)),
        pl.BlockSpec((None, TMO, d), lambda bi, i: (bi, jnp.clip(i - n_lat, 0, n_ctx - 1), 0))]
    return pl.pallas_call(
        body,
        grid=(b, n_tiles),
        in_specs=[tile(col) for col in stream_col] + [vec] * len(vecs)
                 + [pl.BlockSpec((d, d), lambda bi, i: (0, 0))] + res_specs + [_mod_spec(d, 2, row), vec, vec],
        out_specs=tile(0),
        out_shape=jax.ShapeDtypeStruct((b, n_tiles * TMO, d), F32),
        scratch_shapes=[pltpu.VMEM((TMO, d), BF16)],
        compiler_params=_cp("parallel", "parallel"),
        name=name,
    )(*streams, *vecs, w, *res, mod, ln_g, ln_b)


def _mlp_body(h_ref, sc_ref, sh_ref, gate_ref, w1_ref, w2_ref, lg_ref, lb_ref, o_ref, u_ref, acc_ref):
    k = pl.program_id(2)

    @pl.when(k == 0)
    def _():
        u_ref[...] = (h_ref[...] * (1.0 + sc_ref[...]) + sh_ref[...]).astype(BF16)
        acc_ref[...] = jnp.zeros_like(acc_ref)

    a = jnp.maximum(_dot(u_ref[...], w1_ref[...]), 0.0)
    acc_ref[...] += _dot((a * a).astype(BF16), w2_ref[...])

    @pl.when(k == pl.num_programs(2) - 1)
    def _():
        o_ref[...] = _layer_norm(ALPHA * h_ref[...] + gate_ref[...] * acc_ref[...], lg_ref[...], lb_ref[...])


def _mlp(h, mod, w1, w2, ln_g, ln_b, batch, ctx_tile, n_tiles, name):
    b, _, d = h.shape
    dff = w1.shape[1]
    row = lambda bi, i, k: jnp.where(i == ctx_tile, batch, bi)
    tile = pl.BlockSpec((None, TM, d), lambda bi, i, k: (bi, i, 0))
    vec = pl.BlockSpec((1, d), lambda bi, i, k: (0, 0))
    return pl.pallas_call(
        _mlp_body,
        grid=(b, n_tiles, dff // FF_TILE),
        in_specs=[tile, _mod_spec(d, 4, row), _mod_spec(d, 3, row), _mod_spec(d, 5, row),
                  pl.BlockSpec((d, FF_TILE), lambda bi, i, k: (0, k)),
                  pl.BlockSpec((FF_TILE, d), lambda bi, i, k: (k, 0)), vec, vec],
        out_specs=tile,
        out_shape=jax.ShapeDtypeStruct((b, n_tiles * TM, d), F32),
        scratch_shapes=[pltpu.VMEM((TM, d), BF16), pltpu.VMEM((TM, d), F32)],
        compiler_params=_cp("parallel", "parallel", "arbitrary"),
        name=name,
    )(h, mod, mod, mod, w1, w2, ln_g, ln_b)


def _rproj_body(ctx_tile, c_len, n_lat_tiles,
                h_ref, hp_ref, hn_ref, sc_ref, sh_ref, mu_ref, w_ref, wl_ref, o_ref, ol_ref, u_ref, xx_ref):
    i = pl.program_id(1)

    @pl.when(pl.program_id(2) == 0)
    def _():
        is_ctx = i == ctx_tile
        sc = 1.0 + sc_ref[...]
        sh = sh_ref[...]
        u = h_ref[...] * sc + sh
        u_prev = hp_ref[...] * sc + sh
        u_next = hn_ref[...] * sc + sh
        u_ref[...] = u
        t = _iota((TM, 1), 0)
        tw = jnp.bitwise_and(t, GRID_W - 1)
        q = u.shape[1] // 4
        flag = lambda cond: jnp.where(cond, 1, 0)
        ctx_prev, ctx_next = flag(t != 0), flag(t != c_len - 1)
        lat_left, lat_right = flag(tw != 0), flag(tw != GRID_W - 1)
        lat_up = jnp.where(i == 0, flag(t >= GRID_W), 1)
        lat_down = jnp.where(i == n_lat_tiles - 1, flag(t < TM - GRID_W), 1)
        for qi in range(4):
            sl = slice(qi * q, (qi + 1) * q)
            uq = u[:, sl]
            before = pltpu.roll(uq, 1, 0)
            after = pltpu.roll(uq, TM - 1, 0)
            if qi == 0:
                src = before
                keep = jnp.where(is_ctx, ctx_prev, lat_left)
            elif qi == 1:
                src = jnp.where(is_ctx, before, after)
                keep = jnp.where(is_ctx, ctx_prev, lat_right)
            elif qi == 2:
                above = jnp.concatenate([u_prev[:, sl], uq[:TM - GRID_W]], axis=0)
                src = jnp.where(is_ctx, after, above)
                keep = jnp.where(is_ctx, ctx_next, lat_up)
            else:
                below = jnp.concatenate([uq[GRID_W:], u_next[:, sl]], axis=0)
                src = jnp.where(is_ctx, after, below)
                keep = jnp.where(is_ctx, ctx_next, lat_down)
            xx_ref[:, sl] = jnp.where(keep != 0, src, 0.0) - uq

        for ti in range(ol_ref.shape[-1] // R_LORA_PAD):
            sl = slice(ti * R_LORA_PAD, (ti + 1) * R_LORA_PAD)
            x = (u_ref[...] + xx_ref[...] * mu_ref[3 + max(ti - 1, 0)]).astype(BF16)
            ol_ref[:, sl] = _dot(x, wl_ref[:, sl])

    x = (u_ref[...] + xx_ref[...] * mu_ref[pl.program_id(2)]).astype(BF16)
    o_ref[...] = _dot(x, w_ref[...]).astype(o_ref.dtype)


def _rproj(h, mod, mu, w, w_lora, batch, s_len, c_len):
    b, tt, d = h.shape
    n = w.shape[1]
    nl = w_lora.shape[1]
    nj = n // d
    ctx_tile = s_len // TM
    row = lambda bi, i, j: jnp.where(i == ctx_tile, batch, bi)
    r64 = TM // GRID_W
    return pl.pallas_call(
        functools.partial(_rproj_body, ctx_tile, c_len, s_len // TM),
        grid=(b, tt // TM, nj),
        in_specs=[pl.BlockSpec((None, TM, d), lambda bi, i, j: (bi, i, 0)),
                  pl.BlockSpec((None, GRID_W, d), lambda bi, i, j: (bi, jnp.maximum(i * r64 - 1, 0), 0)),
                  pl.BlockSpec((None, GRID_W, d),
                               lambda bi, i, j: (bi, jnp.minimum((i + 1) * r64, tt // GRID_W - 1), 0)),
                  _mod_spec(d, 1, row), _mod_spec(d, 0, row),
                  pl.BlockSpec((8, 1, d), lambda bi, i, j: (0, 0, 0)),
                  pl.BlockSpec((d, d), lambda bi, i, j: (0, j)),
                  pl.BlockSpec((d, nl), lambda bi, i, j: (0, 0))],
        out_specs=[pl.BlockSpec((None, TM, d), lambda bi, i, j: (bi, i, j)),
                   pl.BlockSpec((None, TM, nl), lambda bi, i, j: (bi, i, 0))],
        out_shape=[jax.ShapeDtypeStruct((b, tt, n), BF16), jax.ShapeDtypeStruct((b, tt, nl), F32)],
        scratch_shapes=[pltpu.VMEM((TM, d), F32), pltpu.VMEM((TM, d), F32)],
        compiler_params=_cp("parallel", "parallel", "arbitrary"),
        name="rwkv_proj",
    )(h, h, h, mod, mod, mu, w, w_lora)


def _block_diag(x, mask):
    return jnp.where(mask, jnp.concatenate([x] * (R_GROUP // R_CHUNK), axis=0), 0.0)


def _rloc_body(r_ref, k_ref, v_ref, lo_ref, g2_ref, w2_ref, a2_ref, vec_ref,
               rf_ref, rb_ref, y0_ref, bon_ref, gate_ref, gf_ref, gb_ref, hf_ref, hb_ref, dgf_ref, dgb_ref,
               rs_ref, vs_ref, kk_ref, lw_ref, kd_ref, bb_ref):
    L = R_CHUNK
    W = R_GROUP
    lo = lo_ref[...]
    r = r_ref[...].astype(F32)
    k = k_ref[...].astype(F32)
    v = v_ref[...].astype(F32)
    w0 = (vec_ref[0:1, :], vec_ref[1:2, :])
    a0 = (vec_ref[2:3, :], vec_ref[3:4, :])
    k_k = vec_ref[4:5, :]
    k_a = vec_ref[5:6, :]
    r_k = vec_ref[6:7, :]
    ones = _group_ones(W, R_HEAD)

    gate_ref[...] = _dot(_sigmoid(lo[:, :R_GATE_LORA]).astype(BF16), g2_ref[...]).astype(gate_ref.dtype)
    kraw = k * k_k
    ss = _dot((kraw * kraw).astype(BF16), ones)
    kk = kraw / jnp.maximum(jnp.sqrt(ss), 1e-12)
    rs_ref[...] = r
    vs_ref[...] = v
    kk_ref[...] = kk
    kd_sum = jnp.zeros_like(k)
    for d in range(2):
        wl = lo[:, R_GATE_LORA + d * R_LORA_PAD:R_GATE_LORA + (d + 1) * R_LORA_PAD]
        al = lo[:, R_GATE_LORA + (2 + d) * R_LORA_PAD:R_GATE_LORA + (3 + d) * R_LORA_PAD]
        lwx = w0[d] + _dot(jnp.tanh(wl).astype(BF16), w2_ref[d])
        lw_ref[d] = -_sigmoid(lwx) * (jnp.e ** -0.5)
        iclr = _sigmoid(a0[d] + _dot(al.astype(BF16), a2_ref[d]))
        kd = k * (1.0 + (iclr - 1.0) * k_a)
        kd_ref[d] = kd
        bb_ref[d] = kk * iclr
        kd_sum = kd_sum + kd
    bon_ref[...] = (_dot((r * r_k * kd_sum).astype(BF16), ones) * v).astype(bon_ref.dtype)

    lane_head = _iota((W, W), 1) // R_HEAD
    bd_mask = _iota((W, W), 0) // L == lane_head
    t_i = _iota((L, W), 0)
    s_i = jnp.bitwise_and(_iota((L, W), 1), L - 1)
    eye = jnp.where(t_i == s_i, 1.0, 0.0)
    ts_xor = jnp.bitwise_xor(t_i, s_i)
    diag2 = jnp.right_shift(ts_xor, 1) == 0
    off_masks = [jnp.right_shift(ts_xor, lv) == 1 for lv in range(1, 6)]
    r_i = _iota((L, L), 0)
    c_i = _iota((L, L), 1)
    outs = ((rf_ref, gf_ref, hf_ref), (rb_ref, gb_ref, hb_ref))

    tris = (jnp.where(r_i >= c_i, 1.0, 0.0).astype(BF16), jnp.where(r_i <= c_i, 1.0, 0.0).astype(BF16))
    stricts = (t_i > s_i, t_i < s_i)
    incls = (t_i >= s_i, t_i <= s_i)
    bd = lambda x: _block_diag(x, bd_mask).astype(BF16)

    def head_t(x):
        xt = x.T
        return jnp.concatenate([xt[hd * R_HEAD:(hd + 1) * R_HEAD, :] for hd in range(W // R_HEAD)], axis=1)

    n_chunks = R_BLOCK // L
    inst = [(ci, d) for ci in range(n_chunks) for d in range(2)]
    rows_of = [pl.ds(ci * L, L) for ci in range(n_chunks)]
    vcs = [vs_ref[rows_of[ci], :] for ci in range(n_chunks)]
    bd_vs = [bd(vc) for vc in vcs]

    st = []
    for ci, d in inst:
        rows = rows_of[ci]
        lwc = lw_ref[d, rows, :]
        cum = sum(_dot(tris[d], part) for part in _split3(lwc))
        e_last = L - 1 if d == 0 else 0
        tot = cum[e_last:e_last + 1, :]
        p_inv = jnp.exp(-cum)
        p_end = jnp.exp(tot - cum)
        bc = bb_ref[d, rows, :]
        kdc = kd_ref[d, rows, :]
        st.append(dict(tot=tot, a_t=-kk_ref[rows, :] * jnp.exp(cum - lwc), r_t=rs_ref[rows, :] * jnp.exp(cum),
                       b_inv=bc * p_inv, k_inv=kdc * p_inv,
                       b_end=head_t(bc * p_end).astype(BF16), k_end=head_t(kdc * p_end).astype(BF16)))
    for s, (ci, d) in zip(st, inst):
        z = _dot_nt(jnp.concatenate([s["a_t"], s["r_t"]], axis=0).astype(BF16),
                    jnp.concatenate([bd(s.pop("b_inv")), bd(s.pop("k_inv"))], axis=0))
        s["a_ab"] = jnp.where(stricts[d], z[:L, :W], 0.0)
        s["a_ak"] = jnp.where(stricts[d], z[:L, W:], 0.0).astype(BF16)
        s["m_rb"] = jnp.where(incls[d], z[L:, :W], 0.0).astype(BF16)
        s["m_rk"] = jnp.where(incls[d], z[L:, W:], 0.0).astype(BF16)
        s["tinv"] = eye + jnp.where(diag2, s["a_ab"], 0.0)
    for ci in range(n_chunks):
        pair = [s for s, (cj, d) in zip(st, inst) if cj == ci]
        lhs = [s.pop("a_ak") for s in pair] + [s.pop("m_rk") for s in pair] + [s.pop("k_end") for s in pair]
        xv = _dot(jnp.concatenate(lhs, axis=0), bd_vs[ci])
        for d, s in enumerate(pair):
            s["akv"] = xv[d * L:(d + 1) * L]
            s["mkv"] = xv[(2 + d) * L:(3 + d) * L]
            s["zk"] = xv[(4 + d) * L:(5 + d) * L]
    for off in off_masks:
        for s in st:
            s["x1"] = _dot(s["tinv"].astype(BF16), bd(jnp.where(off, s["a_ab"], 0.0)))
        for s in st:
            s["tinv"] = s["tinv"] + _dot(s.pop("x1").astype(BF16), bd(s["tinv"]))
    for s in st:
        x = _dot(s.pop("tinv").astype(BF16), jnp.concatenate([bd(s.pop("a_t")), bd(s.pop("akv"))], axis=1))
        s["a_p"] = x[:, :W]
        s["u0"] = x[:, W:]
    y0s = [jnp.zeros((L, W), F32) for _ in range(n_chunks)]
    for s, (ci, d) in zip(st, inst):
        yz = _dot(jnp.concatenate([s.pop("m_rb"), s.pop("b_end")], axis=0),
                  jnp.concatenate([bd(s.pop("a_p")), bd(s.pop("u0"))], axis=1))
        outs[d][0][rows_of[ci], :] = (s.pop("r_t") + yz[:L, :W]).astype(BF16)
        y0s[ci] = y0s[ci] + yz[:L, W:] + s.pop("mkv")
        outs[d][1][rows_of[ci], :] = yz[L:, :W].astype(BF16)
        outs[d][2][rows_of[ci], :] = (yz[L:, W:] + s.pop("zk")).astype(BF16)
    for ci in range(n_chunks):
        y0_ref[rows_of[ci], :] = y0s[ci]
    for dg_ref in (dgf_ref, dgb_ref):
        dg_ref[n_chunks:, :] = jnp.zeros((8 - n_chunks, W), F32)
    for s, (ci, d) in zip(st, inst):
        (dgf_ref, dgb_ref)[d][ci:ci + 1, :] = jnp.exp(s["tot"])


def _rloc(rkv, lora, g2, w2p, a2p, vecs, s_len, c_len):
    b, tt, n3 = rkv.shape
    d = n3 // 3
    nb = (s_len + c_len) // R_BLOCK
    ng = d // R_GROUP
    tok = lambda col: pl.BlockSpec((None, R_BLOCK, R_GROUP), lambda bi, i, j: (bi, i, col * ng + j))
    out_tok = pl.BlockSpec((None, R_BLOCK, R_GROUP), lambda bi, i, j: (bi, i, j))
    f32_out = jax.ShapeDtypeStruct((b, s_len + c_len, d), F32)
    bf_out = jax.ShapeDtypeStruct((b, s_len + c_len, d), BF16)
    sc = lambda *shape: pltpu.VMEM(shape, F32)
    return pl.pallas_call(
        _rloc_body,
        grid=(b, nb, ng),
        in_specs=[tok(0), tok(1), tok(2),
                  pl.BlockSpec((None, R_BLOCK, lora.shape[2]), lambda bi, i, j: (bi, i, 0)),
                  pl.BlockSpec((R_GATE_LORA, R_GROUP), lambda bi, i, j: (0, j)),
                  pl.BlockSpec((2, R_LORA_PAD, R_GROUP), lambda bi, i, j: (0, 0, j)),
                  pl.BlockSpec((2, R_LORA_PAD, R_GROUP), lambda bi, i, j: (0, 0, j)),
                  pl.BlockSpec((8, R_GROUP), lambda bi, i, j: (0, j))],
        out_specs=[out_tok] * 9 + [pl.BlockSpec((None, None, 8, R_GROUP), lambda bi, i, j: (bi, i, 0, j))] * 2,
        out_shape=[bf_out, bf_out, f32_out, bf_out, bf_out, bf_out, bf_out, bf_out, bf_out,
                   jax.ShapeDtypeStruct((b, nb, 8, d), F32), jax.ShapeDtypeStruct((b, nb, 8, d), F32)],
        scratch_shapes=[sc(R_BLOCK, R_GROUP), sc(R_BLOCK, R_GROUP), sc(R_BLOCK, R_GROUP),
                        sc(2, R_BLOCK, R_GROUP), sc(2, R_BLOCK, R_GROUP), sc(2, R_BLOCK, R_GROUP)],
        compiler_params=_cp("parallel", "parallel", "arbitrary"),
        name="rwkv_local",
    )(rkv, rkv, rkv, lora, g2, w2p, a2p, vecs)


def _rseq_body(n_lat, n_ctx, gf_ref, gb_ref, hf_ref, hb_ref, rf_ref, rb_ref, dgf_ref, dgb_ref,
               yf_ref, yb_ref, st_ref):
    L = R_CHUNK
    W = R_GROUP
    c = pl.program_id(1)

    @pl.when(c == 0)
    def _():
        st_ref[...] = jnp.zeros_like(st_ref)

    bd_mask = _iota((W, W), 0) // L == _iota((W, W), 1) // R_HEAD
    eye = jnp.where(_iota((L, W), 0) == jnp.bitwise_and(_iota((L, W), 1), L - 1), 1.0, 0.0)
    ins = ((gf_ref, hf_ref, rf_ref, dgf_ref, yf_ref), (gb_ref, hb_ref, rb_ref, dgb_ref, yb_ref))
    for d in range(2):
        g_ref, ha_ref, rp_ref, dg_ref, y_ref = ins[d]
        chunk_in_block = jnp.bitwise_and(_seq_block(c, d, n_lat, n_ctx), R_BLOCK // L - 1)
        for gi in range(st_ref.shape[-1] // W):
            sl = slice(gi * W, (gi + 1) * W)
            st = st_ref[d, :, sl]
            st_hi = st.astype(BF16)
            st_lo = (st - st_hi.astype(F32)).astype(BF16)
            g_hi, g_lo = _split2(g_ref[:, sl].astype(F32) + eye * dg_ref[pl.ds(chunk_in_block, 1), sl])
            rp = rp_ref[:, sl]
            bd_hi = _block_diag(st_hi.astype(F32), bd_mask).astype(BF16)
            bd_lo = _block_diag(st_lo.astype(F32), bd_mask).astype(BF16)
            o1 = _dot(jnp.concatenate([g_hi, g_lo, rp], axis=0), bd_hi)
            o2 = _dot(jnp.concatenate([g_hi, rp], axis=0), bd_lo)
            st_ref[d, :, sl] = o1[:L] + o1[L:2 * L] + o2[:L] + ha_ref[:, sl].astype(F32)
            y_ref[:, sl] = (o1[2 * L:] + o2[L:]).astype(y_ref.dtype)


def _rseq(gf, gb, hf, hb, rf, rb, dgf, dgb, s_len, c_len):
    b, tt, d = gf.shape
    L = R_CHUNK
    n_lat, n_ctx = s_len // L, c_len // L
    per_block = R_BLOCK // L
    spec = lambda direction: pl.BlockSpec(
        (None, L, d), lambda bi, c: (bi, _seq_block(c, direction, n_lat, n_ctx), 0))
    dspec = lambda direction: pl.BlockSpec(
        (None, None, 8, d), lambda bi, c: (bi, _seq_block(c, direction, n_lat, n_ctx) // per_block, 0, 0))
    out = jax.ShapeDtypeStruct((b, tt, d), BF16)
    return pl.pallas_call(
        functools.partial(_rseq_body, n_lat, n_ctx),
        grid=(b, n_lat + n_ctx),
        in_specs=[spec(0), spec(1), spec(0), spec(1), spec(0), spec(1), dspec(0), dspec(1)],
        out_specs=[spec(0), spec(1)],
        out_shape=[out, out],
        scratch_shapes=[pltpu.VMEM((2, L, d), F32)],
        compiler_params=_cp("parallel", "arbitrary"),
        name="rwkv_seq",
    )(gf, gb, hf, hb, rf, rb, dgf, dgb)


def kernel(x, c, ctx, c_ctx, ada_w, ada_b, ln_g, ln_b, mlp_w1, mlp_w2, m_w_in, m_conv, m_gate_b, m_norm_w, m_w_out, r_w_in, r_mu, r_g2, r_w0, r_w2, r_a0, r_a2, r_k_k, r_k_a, r_r_k, r_lnx_w, r_lnx_b, r_w_out):
    batch, s_len, d = x.shape
    c_len = ctx.shape[1]
    assert batch < MOD_ROWS and s_len % TM == 0 and c_len % M_CHUNK == 0 and c_len <= TM
    assert s_len % GRID_W == 0 and d % R_GROUP == 0 and ada_w.shape[0] == DEPTH == 2
    tt = s_len + TM
    ctx_tile = s_len // TM
    row = lambda a: a.reshape(1, -1)

    ctx_pad = jnp.concatenate([ctx, jnp.zeros((batch, TM - c_len, d), F32)], axis=1)
    cvec = jnp.concatenate([c, c_ctx[None], jnp.zeros((MOD_ROWS - batch - 1, d), F32)], axis=0)
    mod = _ada(cvec, ada_w, ada_b).reshape(DEPTH, MOD_ROWS, 1, 6 * d)

    nqk = 2 * M_HEADS * M_DQK
    nv = M_HEADS * M_DV
    w_in = m_w_in[0]
    w_gates = jnp.pad(w_in[:, nqk + 2 * nv:], ((0, 0), (0, 128 - 4 * M_HEADS))).astype(BF16)
    p, gates = _proj(x, ctx_pad, mod[0], w_in[:, :nqk + 2 * nv].astype(BF16), w_gates, d, batch, "mlstm_proj")
    gates_t = jnp.swapaxes(gates[:, :, :4 * M_HEADS], 1, 2)
    gb = m_gate_b[0].reshape(-1)
    gb_row = jnp.pad(gb, (0, 128 - gb.shape[0])).reshape(1, 128)
    gb_col = gb.reshape(-1, 1)
    scan = functools.partial(_mlstm_scan, p, gates, gates_t, m_conv[0], gb_row, gb_col)
    h_f = scan(0, s_len, c_len)
    h_b = scan(1, s_len, c_len)
    h = _out_call(functools.partial(_mlstm_out_body, s_len), (h_f, h_b, p), (0, 0, (nqk + nv) // d),
                  (row(m_norm_w[0]),), m_w_out[0].astype(BF16), (x, ctx_pad), mod[0], row(ln_g[0, 0]), row(ln_b[0, 0]),
                  batch, s_len, tt // TMO, "mlstm_out")
    h = _mlp(h, mod[0], mlp_w1[0].astype(BF16), mlp_w2[0].astype(BF16), row(ln_g[0, 1]), row(ln_b[0, 1]),
             batch, ctx_tile, tt // TM, "mlp0")

    w_in = r_w_in[0]
    mu = r_mu[0].reshape(8, 1, d)
    lo_cols = [w_in[:, 3 * d:3 * d + R_GATE_LORA]]
    for gi in range(4):
        start = 3 * d + R_GATE_LORA + gi * R_LORA
        lo_cols.append(jnp.pad(w_in[:, start:start + R_LORA], ((0, 0), (0, R_LORA_PAD - R_LORA))))
    w_lora = jnp.concatenate(lo_cols, axis=1).astype(BF16)
    rkv, lora = _rproj(h, mod[1], mu, w_in[:, :3 * d].astype(BF16), w_lora, batch, s_len, c_len)
    pad_rows = ((0, 0), (0, R_LORA_PAD - R_LORA), (0, 0))
    vecs = jnp.concatenate([r_w0[0], r_a0[0], row(r_k_k[0]), row(r_k_a[0]), row(r_r_k[0]),
                            jnp.zeros((1, d), F32)], axis=0)
    rf, rb, y0, bonus, gate, g_f, g_b, ha_f, ha_b, dg_f, dg_b = _rloc(
        rkv, lora, r_g2[0].astype(BF16), jnp.pad(r_w2[0], pad_rows).astype(BF16),
        jnp.pad(r_a2[0], pad_rows).astype(BF16), vecs, s_len, c_len)
    y_f, y_b = _rseq(g_f, g_b, ha_f, ha_b, rf, rb, dg_f, dg_b, s_len, c_len)
    h1 = _out_call(_rwkv_out_body, (y_f, y_b, y0, bonus, gate), (0, 0, 0, 0, 0),
                   (row(r_lnx_w[0]), row(r_lnx_b[0])), r_w_out[0].astype(BF16), h, mod[1],
                   row(ln_g[1, 0]), row(ln_b[1, 0]), batch, s_len, s_len // TMO, "rwkv_out")
    return _mlp(h1, mod[1], mlp_w1[1].astype(BF16), mlp_w2[1].astype(BF16), row(ln_g[1, 1]), row(ln_b[1, 1]),
                batch, ctx_tile, s_len // TM, "mlp1")
```
